```python
import jax, jax.numpy as jnp
from jax import lax
import numpy as np

D_MODEL = 2048
BATCH = 4
SEQ = 4096
DEPTH = 1

GRID_W = 64
CTX_LEN = 256
EPS = 1e-6

M_HEADS = 8
M_DQK = 128
M_DV = 256
M_CONV = 5
M_CHUNK = 64
FGATE_BIAS_LO = 3.0
FGATE_BIAS_HI = 6.0

A_HEADS = 16
A_NOPE = 128
A_ROPE = 64
A_QK = A_NOPE + A_ROPE
A_DV = 128
KV_RANK = 512
ROPE_FREQS = A_ROPE // 4
ROPE_THETA = 10000.0
Q_BLOCK = 128

M_QK_W = M_HEADS * M_DQK
M_V_W = M_HEADS * M_DV
M_GATE_W = 4 * M_HEADS
A_Q_W = A_HEADS * A_QK
A_V_W = A_HEADS * A_DV
KV_SIZES = (M_QK_W, M_V_W, M_GATE_W, KV_RANK, A_ROPE)
Q_SIZES = (M_QK_W, M_V_W, M_V_W, A_Q_W, A_V_W, 2 * D_MODEL)
KV_COLS = M_QK_W + M_V_W + M_GATE_W + KV_RANK + A_ROPE
IN_COLS = KV_COLS + M_QK_W + M_V_W + M_V_W + A_Q_W + A_V_W + 2 * D_MODEL

kernel_name = "hybrid_mlstm_mla_prefix_block"


def split_cols(a, sizes):
    idx = np.cumsum(sizes)[:-1].tolist()
    return jnp.split(a, idx, axis=-1)


def rmsnorm(x, g):
    xf = x.astype(jnp.float32)
    y = xf * lax.rsqrt(jnp.mean(xf * xf, axis=-1, keepdims=True) + EPS)
    return (y * g.astype(jnp.float32)).astype(x.dtype)


def adaln(cvec, ada_w, ada_b):
    mod = jax.nn.silu(cvec) @ ada_w + ada_b
    return jnp.split(mod, 3, axis=-1)


def flip_seq(a):
    return jnp.flip(a, axis=2)


def centred_dwconv(x, w, b):
    k, ch = w.shape
    y = lax.conv_general_dilated(x, w[:, None, :].astype(x.dtype), window_strides=(1,),
                                 padding=[(k // 2, k // 2)],
                                 dimension_numbers=("NWC", "WIO", "NWC"),
                                 feature_group_count=ch)
    return y + b


def mlstm_heads(a, dh):
    b, t, _ = a.shape
    return a.reshape(b, t, -1, dh).transpose(0, 2, 1, 3)


def mlstm_qk(raw, w, b):
    return mlstm_heads(jax.nn.silu(centred_dwconv(raw, w, b)), M_DQK)


def mlstm_gates(raw, gate_b):
    b, t, _ = raw.shape
    g = (raw.astype(jnp.float32) + gate_b.astype(jnp.float32)).reshape(b, t, 4, M_HEADS)
    g = jnp.transpose(g, (2, 0, 3, 1))
    return (g[0], jax.nn.log_sigmoid(g[1]), g[2], jax.nn.log_sigmoid(g[3]))


def mlstm_zero_state(b):
    return (jnp.zeros((b, M_HEADS, M_DQK, M_DV), jnp.float32),
            jnp.zeros((b, M_HEADS, M_DQK), jnp.float32),
            jnp.zeros((b, M_HEADS), jnp.float32))


def mlstm_final_state(k, v, ig, lf):
    k = k.astype(jnp.float32)
    v = v.astype(jnp.float32)
    bcum = jnp.cumsum(lf, axis=-1)
    w_s = bcum[..., -1:] - bcum + ig
    m = jnp.max(w_s, axis=-1)
    ws = jnp.exp(w_s - m[..., None])
    c_state = jnp.einsum("bhs,bhsd,bhsv->bhdv", ws, k, v)
    n_state = jnp.einsum("bhs,bhsd->bhd", ws, k)
    return (c_state, n_state, m)


def mlstm_chunkwise(q, k, v, ig, lf, state0):
    b, h, t, _ = q.shape
    nc = t // M_CHUNK

    def to_chunks(a):
        a = a.astype(jnp.float32)
        return jnp.moveaxis(a.reshape(a.shape[:2] + (nc, M_CHUNK) + a.shape[3:]), 2, 0)

    xs = (to_chunks(q * (M_DQK ** -0.5)), to_chunks(k), to_chunks(v), to_chunks(ig), to_chunks(lf))
    mask = jnp.tril(jnp.ones((M_CHUNK, M_CHUNK), dtype=bool))

    def step(carry, inp):
        c_prev, n_prev, m_prev = carry
        qc, kc, vc, ic, fc = inp
        bcum = jnp.cumsum(fc, axis=-1)
        d = bcum[..., :, None] - bcum[..., None, :] + ic[..., None, :]
        d = jnp.where(mask, d, -jnp.inf)
        inter = bcum + m_prev[..., None]
        m_row = jnp.maximum(inter, jnp.max(d, axis=-1))
        s_inter = jnp.exp(inter - m_row)
        qk = jnp.einsum("bhtd,bhsd->bhts", qc, kc) * jnp.exp(d - m_row[..., None])
        num = (jnp.einsum("bhts,bhsv->bhtv", qk, vc)
               + s_inter[..., None] * jnp.einsum("bhtd,bhdv->bhtv", qc, c_prev))
        den = jnp.sum(qk, axis=-1) + s_inter * jnp.einsum("bhtd,bhd->bht", qc, n_prev)
        h_out = num / jnp.maximum(jnp.abs(den), jnp.exp(-m_row))[..., None]
        b_tot = bcum[..., -1]
        w_s = b_tot[..., None] - bcum + ic
        m_new = jnp.maximum(b_tot + m_prev, jnp.max(w_s, axis=-1))
        decay = jnp.exp(b_tot + m_prev - m_new)
        ws = jnp.exp(w_s - m_new[..., None])
        c_new = decay[..., None, None] * c_prev + jnp.einsum("bhs,bhsd,bhsv->bhdv", ws, kc, vc)
        n_new = decay[..., None] * n_prev + jnp.einsum("bhs,bhsd->bhd", ws, kc)
        return (c_new, n_new, m_new), h_out

    _, hs = lax.scan(step, state0, xs)
    return jnp.moveaxis(hs, 0, 2).reshape(b, h, t, M_DV)


def mlstm_bidir(q, k, v, gates, state_f, state_b):
    ig_f, lf_f, ig_b, lf_b = gates
    h_f = mlstm_chunkwise(q, k, v, ig_f, lf_f, state_f)
    h_b = flip_seq(mlstm_chunkwise(flip_seq(q), flip_seq(k), flip_seq(v),
                                   flip_seq(ig_b), flip_seq(lf_b), state_b))
    return h_f + h_b


def axial_angles(rows):
    row = jnp.repeat(jnp.arange(rows), GRID_W).astype(jnp.float32)
    col = jnp.tile(jnp.arange(GRID_W), rows).astype(jnp.float32)
    freqs = ROPE_THETA ** (-jnp.arange(ROPE_FREQS, dtype=jnp.float32) / ROPE_FREQS)
    return jnp.stack([row[:, None] * freqs, col[:, None] * freqs], axis=1)


def axial_rope(t, ang):
    b, s, h, _ = t.shape
    nope, rope = t[..., :A_NOPE], t[..., A_NOPE:]
    r = rope.reshape(b, s, h, 2, 2, ROPE_FREQS)
    cos = jnp.cos(ang)[None, :, None].astype(t.dtype)
    sin = jnp.sin(ang)[None, :, None].astype(t.dtype)
    x1, x2 = r[..., 0, :], r[..., 1, :]
    rot = jnp.stack([x1 * cos - x2 * sin, x1 * sin + x2 * cos], axis=-2).reshape(b, s, h, A_ROPE)
    return jnp.concatenate([nope, rot], axis=-1)


def mla_kv(ckv, k_rope, kv_norm_g, w_uk, w_uv, k_norm_g):
    b, t, _ = ckv.shape
    cn = rmsnorm(ckv, kv_norm_g)
    k_nope = (cn @ w_uk).reshape(b, t, A_HEADS, A_NOPE)
    v = (cn @ w_uv).reshape(b, t, A_HEADS, A_DV)
    k_r = jnp.broadcast_to(k_rope[:, :, None, :], (b, t, A_HEADS, A_ROPE))
    k = rmsnorm(jnp.concatenate([k_nope, k_r], axis=-1), k_norm_g)
    return k, v


def mla_q(qa, q_norm_g):
    b, t, _ = qa.shape
    return rmsnorm(qa.reshape(b, t, A_HEADS, A_QK), q_norm_g)


def block_attention(q, k, v):
    b, s, h, dh = q.shape
    nb = s // Q_BLOCK
    qb = jnp.moveaxis(q.reshape(b, nb, Q_BLOCK, h, dh), 1, 0)
    scale = dh ** -0.5

    def one_block(qblk):
        sc = jnp.einsum("bqhd,bkhd->bhqk", qblk, k, preferred_element_type=jnp.float32) * scale
        p = jax.nn.softmax(sc, axis=-1).astype(v.dtype)
        return jnp.einsum("bhqk,bkhd->bqhd", p, v)

    o = lax.map(one_block, qb)
    return jnp.moveaxis(o, 0, 1).reshape(b, s, h * v.shape[-1])


def merge_branches(h_m, o_gate, z_m, o_a, z_a, g_merge, mh_norm_g, w_proj_m, w_proj_a, w_out):
    b, _, t, _ = h_m.shape
    hm = rmsnorm(h_m.transpose(0, 2, 1, 3), mh_norm_g.reshape(M_HEADS, M_DV)).reshape(b, t, M_V_W)
    hm = hm.astype(z_m.dtype) * jax.nn.sigmoid(o_gate) * jax.nn.silu(z_m)
    p_m = hm @ w_proj_m
    p_a = (o_a * jax.nn.silu(z_a)) @ w_proj_a
    g_m, g_a = jnp.split(jax.nn.sigmoid(g_merge), 2, axis=-1)
    return (g_m * p_m + g_a * p_a) @ w_out


def hybrid_layer(x, ctx, c, c_ctx, ada_w, ada_b, norm_g, w_in, conv_w, conv_b, gate_b, mh_norm_g,
                 q_norm_g, k_norm_g, kv_norm_g, w_uk, w_uv, w_proj_m, w_proj_a, w_out, ang, update_ctx):
    shift, scale, gate = adaln(c, ada_w, ada_b)
    shift_c, scale_c, gate_c = adaln(c_ctx, ada_w, ada_b)
    h = rmsnorm(x, norm_g) * (1 + scale[:, None]) + shift[:, None]
    hc = rmsnorm(ctx, norm_g) * (1 + scale_c) + shift_c
    proj = h @ w_in
    proj_c = hc @ (w_in if update_ctx else w_in[:, :KV_COLS])
    cw_q, cw_k = conv_w[:, :M_QK_W], conv_w[:, M_QK_W:]
    cb_q, cb_k = conv_b[:M_QK_W], conv_b[M_QK_W:]

    km_c, vm_c, gt_c, ckv_c, kr_c = split_cols(proj_c[..., :KV_COLS], KV_SIZES)
    k_mc = mlstm_qk(km_c, cw_k, cb_k)
    v_mc = mlstm_heads(vm_c, M_DV)
    gates_c = mlstm_gates(gt_c, gate_b)
    state_f = mlstm_final_state(k_mc, v_mc, gates_c[0], gates_c[1])
    state_b = mlstm_final_state(flip_seq(k_mc), flip_seq(v_mc), flip_seq(gates_c[2]), flip_seq(gates_c[3]))
    k_ac, v_ac = mla_kv(ckv_c, kr_c, kv_norm_g, w_uk, w_uv, k_norm_g)

    km, vm, gt, ckv, kr = split_cols(proj[..., :KV_COLS], KV_SIZES)
    qm, om, zm, qa, za, gm = split_cols(proj[..., KV_COLS:], Q_SIZES)
    h_m = mlstm_bidir(mlstm_qk(qm, cw_q, cb_q), mlstm_qk(km, cw_k, cb_k), mlstm_heads(vm, M_DV),
                      mlstm_gates(gt, gate_b), state_f, state_b)
    k_al, v_al = mla_kv(ckv, kr, kv_norm_g, w_uk, w_uv, k_norm_g)
    k_al = axial_rope(k_al, ang)
    q_al = axial_rope(mla_q(qa, q_norm_g), ang)
    o_a = block_attention(q_al, jnp.concatenate([k_ac, k_al], axis=1),
                          jnp.concatenate([v_ac, v_al], axis=1))
    x = x + gate[:, None] * merge_branches(h_m, om, zm, o_a, za, gm, mh_norm_g, w_proj_m, w_proj_a, w_out)

    if update_ctx:
        qm_c, om_c, zm_c, qa_c, za_c, gm_c = split_cols(proj_c[..., KV_COLS:], Q_SIZES)
        zero = mlstm_zero_state(ctx.shape[0])
        h_mc = mlstm_bidir(mlstm_qk(qm_c, cw_q, cb_q), k_mc, v_mc, gates_c, zero, zero)
        o_ac = block_attention(mla_q(qa_c, q_norm_g), k_ac, v_ac)
        ctx = ctx + gate_c * merge_branches(h_mc, om_c, zm_c, o_ac, za_c, gm_c, mh_norm_g,
                                            w_proj_m, w_proj_a, w_out)
    return x, ctx


def setup_inputs(seed: int = 0) -> dict:
    key = jax.random.key(seed)
    ks = jax.random.split(key, 24)

    def nrm(k, shape, s):
        return jax.random.normal(k, shape, jnp.float32) * s

    fbias = jnp.linspace(FGATE_BIAS_LO, FGATE_BIAS_HI, M_HEADS, dtype=jnp.float32)
    gate_b = jnp.concatenate([
        nrm(ks[10], (DEPTH, M_HEADS), 0.1),
        fbias + nrm(ks[11], (DEPTH, M_HEADS), 0.1),
        nrm(ks[12], (DEPTH, M_HEADS), 0.1),
        fbias + nrm(ks[13], (DEPTH, M_HEADS), 0.1)], axis=-1)
    return {
        "x": nrm(ks[0], (BATCH, SEQ, D_MODEL), 1.0),
        "c": nrm(ks[1], (BATCH, D_MODEL), 1.0),
        "ctx": nrm(ks[2], (BATCH, CTX_LEN, D_MODEL), 1.0),
        "c_ctx": nrm(ks[3], (D_MODEL,), 1.0),
        "ada_w": nrm(ks[4], (DEPTH, D_MODEL, 3 * D_MODEL), 0.5 * D_MODEL ** -0.5),
        "ada_b": nrm(ks[5], (DEPTH, 3 * D_MODEL), 0.02),
        "norm_g": 1.0 + nrm(ks[6], (DEPTH, D_MODEL), 0.02),
        "w_in": nrm(ks[7], (DEPTH, D_MODEL, IN_COLS), D_MODEL ** -0.5),
        "conv_w": nrm(ks[8], (DEPTH, M_CONV, 2 * M_QK_W), M_CONV ** -0.5),
        "conv_b": nrm(ks[9], (DEPTH, 2 * M_QK_W), 0.02),
        "gate_b": gate_b,
        "mh_norm_g": 1.0 + nrm(ks[14], (DEPTH, M_V_W), 0.02),
        "q_norm_g": 1.0 + nrm(ks[15], (DEPTH, A_QK), 0.02),
        "k_norm_g": 1.0 + nrm(ks[16], (DEPTH, A_QK), 0.02),
        "kv_norm_g": 1.0 + nrm(ks[17], (DEPTH, KV_RANK), 0.02),
        "w_uk": nrm(ks[18], (DEPTH, KV_RANK, A_HEADS * A_NOPE), KV_RANK ** -0.5),
        "w_uv": nrm(ks[19], (DEPTH, KV_RANK, A_V_W), KV_RANK ** -0.5),
        "w_proj_m": nrm(ks[20], (DEPTH, M_V_W, D_MODEL), M_V_W ** -0.5),
        "w_proj_a": nrm(ks[21], (DEPTH, A_V_W, D_MODEL), A_V_W ** -0.5),
        "w_out": nrm(ks[22], (DEPTH, D_MODEL, D_MODEL), D_MODEL ** -0.5),
    }


def reference(x, c, ctx, c_ctx, ada_w, ada_b, norm_g, w_in, conv_w, conv_b, gate_b, mh_norm_g,
              q_norm_g, k_norm_g, kv_norm_g, w_uk, w_uv, w_proj_m, w_proj_a, w_out):
    rows = x.shape[1] // GRID_W
    ang = axial_angles(rows)
    for layer in range(DEPTH):
        x, ctx = hybrid_layer(x, ctx, c, c_ctx, ada_w[layer], ada_b[layer], norm_g[layer], w_in[layer],
                              conv_w[layer], conv_b[layer], gate_b[layer], mh_norm_g[layer],
                              q_norm_g[layer], k_norm_g[layer], kv_norm_g[layer], w_uk[layer],
                              w_uv[layer], w_proj_m[layer], w_proj_a[layer], w_out[layer], ang,
                              layer + 1 < DEPTH)
    return x
```

```python
import functools

import numpy as np
import jax
import jax.numpy as jnp
from jax import lax
from jax.experimental import pallas as pl
from jax.experimental.pallas import tpu as pltpu

F32 = jnp.float32
BF16 = jnp.bfloat16

D_MODEL = 2048
GRID_W = 64
EPS = 1e-6

M_HEADS = 8
M_DQK = 128
M_DV = 256
M_CONV = 5
A_HEADS = 16
A_NOPE = 128
A_ROPE = 64
A_QK = A_NOPE + A_ROPE
A_DV = 128
KV_RANK = 512
ROPE_FREQS = A_ROPE // 4
ROPE_THETA = 10000.0

M_QK_W = M_HEADS * M_DQK
M_V_W = M_HEADS * M_DV
M_GATE_W = 4 * M_HEADS
A_Q_W = A_HEADS * A_QK
A_V_W = A_HEADS * A_DV

_O_KM = 0
_O_VM = _O_KM + M_QK_W
_O_GT = _O_VM + M_V_W
_O_CKV = _O_GT + M_GATE_W
_O_KR = _O_CKV + KV_RANK
_O_QM = _O_KR + A_ROPE
_O_OM = _O_QM + M_QK_W
_O_ZM = _O_OM + M_V_W
_O_QA = _O_ZM + M_V_W
_O_ZA = _O_QA + A_Q_W
_O_GM = _O_ZA + A_V_W
_O_END = _O_GM + 2 * D_MODEL

P_VM = 0
P_OM = P_VM + M_V_W
P_ZM = P_OM + M_V_W
P_ZA = P_ZM + M_V_W
P_GM = P_ZA + A_V_W
P_QAN = P_GM + 2 * D_MODEL
P_KM = P_QAN + A_HEADS * A_NOPE
P_QM = P_KM + M_QK_W
P_QAR = P_QM + M_QK_W
P_CKV = P_QAR + A_HEADS * A_ROPE
P_COLS = P_CKV + KV_RANK
C_VM = 0
C_KM = C_VM + M_V_W
C_CKV = C_KM + M_QK_W
C_COLS = C_CKV + KV_RANK
S_GT = 0
S_KR = M_GATE_W
S_COLS = 128

LANES = 128
MLSTM_CHUNK = 256
NEG_BIG = -1e30
VMEM_LIMIT = 60 * 1024 * 1024


def _cparams(sem):
    return pltpu.CompilerParams(dimension_semantics=sem, vmem_limit_bytes=VMEM_LIMIT)


def _silu(a):
    return a * jax.nn.sigmoid(a)


def _adaln_kernel(c_ref, w_ref, b_ref, o_ref):
    s = _silu(c_ref[...])
    o_ref[...] = jnp.dot(s.astype(BF16), w_ref[...].astype(BF16), preferred_element_type=F32) + b_ref[...]


def _adaln(c8, ada_w, ada_b):
    n = ada_w.shape[1]
    tn = 1024
    return pl.pallas_call(
        _adaln_kernel,
        grid=(n // tn,),
        in_specs=[pl.BlockSpec((8, D_MODEL), lambda j: (0, 0)),
                  pl.BlockSpec((D_MODEL, tn), lambda j: (0, j)),
                  pl.BlockSpec((1, tn), lambda j: (0, j))],
        out_specs=pl.BlockSpec((8, tn), lambda j: (0, j)),
        out_shape=jax.ShapeDtypeStruct((8, n), F32),
        compiler_params=_cparams(("parallel",)),
        name="adaln",
    )(c8, ada_w, ada_b.reshape(1, n))


def _proj_kernel(x_ref, sc_ref, sh_ref, g_ref, w_ref, ws_ref, o_ref, os_ref, h_ref):
    @pl.when(pl.program_id(2) == 0)
    def _():
        x = x_ref[...]
        ms = jnp.mean(x * x, axis=-1, keepdims=True)
        y = x * lax.rsqrt(ms + EPS) * g_ref[...]
        h = (y * (1.0 + sc_ref[...]) + sh_ref[...]).astype(BF16)
        h_ref[...] = h
        os_ref[...] = jnp.dot(h, ws_ref[...], preferred_element_type=F32)

    o_ref[...] = jnp.dot(h_ref[...], w_ref[...], preferred_element_type=F32).astype(o_ref.dtype)


def _proj(x, scale, shift, norm_g, w_main, w_small, n_cols, tn):
    b, t, _ = x.shape
    tm = min(1024, t)
    return pl.pallas_call(
        _proj_kernel,
        grid=(b, t // tm, n_cols // tn),
        in_specs=[pl.BlockSpec((None, tm, D_MODEL), lambda bi, i, j: (bi, i, 0)),
                  pl.BlockSpec((None, 1, D_MODEL), lambda bi, i, j: (bi, 0, 0)),
                  pl.BlockSpec((None, 1, D_MODEL), lambda bi, i, j: (bi, 0, 0)),
                  pl.BlockSpec((1, D_MODEL), lambda bi, i, j: (0, 0)),
                  pl.BlockSpec((D_MODEL, tn), lambda bi, i, j: (0, j)),
                  pl.BlockSpec((D_MODEL, S_COLS), lambda bi, i, j: (0, 0))],
        out_specs=[pl.BlockSpec((None, tm, tn), lambda bi, i, j: (bi, i, j)),
                   pl.BlockSpec((None, tm, S_COLS), lambda bi, i, j: (bi, i, 0))],
        out_shape=[jax.ShapeDtypeStruct((b, t, n_cols), BF16),
                   jax.ShapeDtypeStruct((b, t, S_COLS), F32)],
        scratch_shapes=[pltpu.VMEM((tm, D_MODEL), BF16)],
        compiler_params=_cparams(("parallel", "parallel", "arbitrary")),
        name="proj",
    )(x, scale, shift, norm_g.reshape(1, D_MODEL), w_main, w_small)


CONV_HALO = 16


def _conv_kernel(prev_ref, cur_ref, next_ref, w_ref, b_ref, *refs, tt, nt, out_scale, emit_t):
    if emit_t:
        o_ref, ot_ref, buf_ref = refs
    else:
        o_ref, buf_ref = refs
    i = pl.program_id(1)
    buf_ref[CONV_HALO:CONV_HALO + tt, :] = cur_ref[...].astype(F32)
    buf_ref[0:CONV_HALO, :] = prev_ref[...].astype(F32) * (i > 0).astype(F32)
    buf_ref[CONV_HALO + tt:2 * CONV_HALO + tt, :] = next_ref[...].astype(F32) * (i < nt - 1).astype(F32)
    acc = jnp.broadcast_to(b_ref[...], o_ref.shape)
    for k in range(M_CONV):
        lo = CONV_HALO - M_CONV // 2 + k
        acc = acc + w_ref[k:k + 1, :] * buf_ref[lo:lo + tt, :]
    y = _silu(acc)
    if out_scale != 1.0:
        y = y * out_scale
    o_ref[...] = y.astype(o_ref.dtype)
    if emit_t:
        ot_ref[...] = y.T.astype(ot_ref.dtype)


def _conv(src, col_off, w, bias, out_scale, emit_t):
    b, t, _ = src.shape
    cw = 512
    tt = min(512, t)
    nt = t // tt
    cb = col_off // cw
    hb = tt // CONV_HALO
    nhalo = t // CONV_HALO
    out_shape = [jax.ShapeDtypeStruct((b, t, M_QK_W), BF16)]
    out_specs = [pl.BlockSpec((None, tt, cw), lambda bi, i, c: (bi, i, c))]
    if emit_t:
        out_shape.append(jax.ShapeDtypeStruct((b, M_QK_W, t), BF16))
        out_specs.append(pl.BlockSpec((None, cw, tt), lambda bi, i, c: (bi, c, i)))
    return pl.pallas_call(
        functools.partial(_conv_kernel, tt=tt, nt=nt, out_scale=out_scale, emit_t=emit_t),
        grid=(b, nt, M_QK_W // cw),
        in_specs=[pl.BlockSpec((None, CONV_HALO, cw), lambda bi, i, c: (bi, jnp.maximum(i * hb - 1, 0), cb + c)),
                  pl.BlockSpec((None, tt, cw), lambda bi, i, c: (bi, i, cb + c)),
                  pl.BlockSpec((None, CONV_HALO, cw), lambda bi, i, c: (bi, jnp.minimum((i + 1) * hb, nhalo - 1), cb + c)),
                  pl.BlockSpec((M_CONV, cw), lambda bi, i, c: (0, c)),
                  pl.BlockSpec((1, cw), lambda bi, i, c: (0, c))],
        out_specs=out_specs,
        out_shape=out_shape,
        scratch_shapes=[pltpu.VMEM((tt + 2 * CONV_HALO, cw), F32)],
        compiler_params=_cparams(("parallel", "parallel", "parallel")),
        name="conv_t" if emit_t else "conv",
    )(src, src, src, w, bias.reshape(1, M_QK_W))


def _mlstm_unit(q, k, kt, v, g_col, g_row, ig_col, ig_row, b_tot, causal, c_ref, n_ref, m_ref, idx, emit):
    L = k.shape[0]
    m_prev = m_ref[idx][0:1, 0:1]
    c_prev = c_ref[idx]
    n_prev = n_ref[idx][0:1, :]
    h_out = None
    if emit:
        rows = lax.broadcasted_iota(jnp.int32, (L, L), 0)
        cols = lax.broadcasted_iota(jnp.int32, (L, L), 1)
        mask = (cols <= rows) if causal else (cols >= rows)
        d = jnp.where(mask, g_col - g_row + ig_row, NEG_BIG)
        inter = g_col + m_prev
        m_row = jnp.maximum(inter, jnp.max(d, axis=-1, keepdims=True))
        s_inter = jnp.exp(inter - m_row)
        s = jnp.dot(q, kt, preferred_element_type=F32) * jnp.exp(d - m_row)
        num = (jnp.dot(s.astype(BF16), v, preferred_element_type=F32)
               + s_inter * jnp.dot(q, c_prev.astype(BF16), preferred_element_type=F32))
        den = (jnp.sum(s, axis=-1, keepdims=True)
               + s_inter * jnp.sum(q.astype(F32) * n_prev, axis=-1, keepdims=True))
        h_out = num / jnp.maximum(jnp.abs(den), jnp.exp(-m_row))
    w_row = b_tot - g_row + ig_row
    w_col = b_tot - g_col + ig_col
    m_new = jnp.maximum(b_tot + m_prev, jnp.max(w_row, axis=-1, keepdims=True))
    decay = jnp.exp(b_tot + m_prev - m_new)
    ws_row = jnp.exp(w_row - m_new)
    ws_col = jnp.exp(w_col - m_new)
    c_ref[idx] = decay * c_prev + jnp.dot((kt.astype(F32) * ws_row).astype(BF16), v, preferred_element_type=F32)
    n_new = decay * n_prev + jnp.sum(k.astype(F32) * ws_col, axis=0, keepdims=True)
    n_ref[idx] = jnp.broadcast_to(n_new, n_ref.shape[1:])
    m_ref[idx] = jnp.broadcast_to(m_new, m_ref.shape[1:])
    return h_out


def _mlstm_gates(gt_ref, gb_ref, tri):
    a = gt_ref[...] + gb_ref[...]
    lane = lax.broadcasted_iota(jnp.int32, a.shape, 1)
    is_forget = ((lane // M_HEADS) % 2) == 1
    act = jnp.where(is_forget, jax.nn.log_sigmoid(a), a)
    cum = jnp.dot(tri, act, preferred_element_type=F32, precision=lax.Precision.HIGHEST)
    return act, cum, act.T, cum.T


def _mlstm_kernel(qf_ref, kf_ref, ktf_ref, vf_ref, gf_ref,
                  qb_ref, kb_ref, ktb_ref, vb_ref, gb_ref,
                  kc_ref, ktc_ref, vc_ref, gc_ref, gbias_ref,
                  hf_ref, hb_ref, c_ref, n_ref, m_ref):
    i = pl.program_id(1)
    L = MLSTM_CHUNK
    rows = lax.broadcasted_iota(jnp.int32, (L, L), 0)
    cols = lax.broadcasted_iota(jnp.int32, (L, L), 1)
    tri_lo = (cols <= rows).astype(F32)
    tri_up = (cols >= rows).astype(F32)

    def run(q_ref, k_ref, kt_ref, v_ref, g_ref, h_ref, direction, emit):
        causal = direction == 0
        act, cum, act_t, cum_t = _mlstm_gates(g_ref, gbias_ref, tri_lo if causal else tri_up)
        for hd in range(M_HEADS):
            ci = 2 * M_HEADS * direction + hd
            cf = ci + M_HEADS
            g_col = cum[:, cf:cf + 1]
            g_row = cum_t[cf:cf + 1, :]
            ig_col = act[:, ci:ci + 1]
            ig_row = act_t[ci:ci + 1, :]
            b_tot = g_col[L - 1:L, :] if causal else g_col[0:1, :]
            ks = slice(hd * M_DQK, (hd + 1) * M_DQK)
            vs = slice(hd * M_DV, (hd + 1) * M_DV)
            q = q_ref[:, ks] if emit else None
            h = _mlstm_unit(q, k_ref[:, ks], kt_ref[ks, :], v_ref[:, vs], g_col, g_row, ig_col, ig_row,
                            b_tot, causal, c_ref, n_ref, m_ref, direction * M_HEADS + hd, emit)
            if emit:
                h_ref[:, vs] = h.astype(h_ref.dtype)

    @pl.when(i == 0)
    def _():
        c_ref[...] = jnp.zeros_like(c_ref)
        n_ref[...] = jnp.zeros_like(n_ref)
        m_ref[...] = jnp.zeros_like(m_ref)
        run(None, kc_ref, ktc_ref, vc_ref, gc_ref, None, 0, False)
        run(None, kc_ref, ktc_ref, vc_ref, gc_ref, None, 1, False)

    @pl.when(i > 0)
    def _():
        run(qf_ref, kf_ref, ktf_ref, vf_ref, gf_ref, hf_ref, 0, True)
        run(qb_ref, kb_ref, ktb_ref, vb_ref, gb_ref, hb_ref, 1, True)


def _mlstm(q, k, kt, proj, small, kc, ktc, proj_c, small_c, gate_bias):
    b, t, _ = q.shape
    L = MLSTM_CHUNK
    nc = t // L
    fwd = lambda bi, i: (bi, jnp.maximum(i - 1, 0), 0)
    bwd = lambda bi, i: (bi, jnp.minimum(nc - i, nc - 1), 0)
    fwd_t = lambda bi, i: (bi, 0, jnp.maximum(i - 1, 0))
    bwd_t = lambda bi, i: (bi, 0, jnp.minimum(nc - i, nc - 1))
    ctx = lambda bi, i: (bi, 0, 0)

    def specs(rm, tm_):
        return [pl.BlockSpec((None, L, M_QK_W), rm), pl.BlockSpec((None, L, M_QK_W), rm),
                pl.BlockSpec((None, M_QK_W, L), tm_), pl.BlockSpec((None, L, M_V_W), rm),
                pl.BlockSpec((None, L, S_COLS), rm)]

    in_specs = (specs(fwd, fwd_t) + specs(bwd, bwd_t)
                + [pl.BlockSpec((None, L, M_QK_W), ctx), pl.BlockSpec((None, M_QK_W, L), ctx),
                   pl.BlockSpec((None, L, M_V_W), ctx), pl.BlockSpec((None, L, S_COLS), ctx),
                   pl.BlockSpec((1, S_COLS), lambda bi, i: (0, 0))])
    return pl.pallas_call(
        _mlstm_kernel,
        grid=(b, nc + 1),
        in_specs=in_specs,
        out_specs=[pl.BlockSpec((None, L, M_V_W), fwd), pl.BlockSpec((None, L, M_V_W), bwd)],
        out_shape=[jax.ShapeDtypeStruct((b, t, M_V_W), BF16)] * 2,
        scratch_shapes=[pltpu.VMEM((2 * M_HEADS, M_DQK, M_DV), F32),
                        pltpu.VMEM((2 * M_HEADS, 8, M_DQK), F32),
                        pltpu.VMEM((2 * M_HEADS, 8, LANES), F32)],
        compiler_params=_cparams(("parallel", "arbitrary")),
        name="mlstm",
    )(q, k, kt, proj, small, q, k, kt, proj, small, kc, ktc, proj_c, small_c, gate_bias)


def _rope(x, cos_ref, sin_ref):
    width = x.shape[-1]
    lane = lax.broadcasted_iota(jnp.int32, x.shape, 1)
    first_half = (lane % (2 * ROPE_FREQS)) < ROPE_FREQS
    partner = jnp.where(first_half, pltpu.roll(x, width - ROPE_FREQS, 1), pltpu.roll(x, ROPE_FREQS, 1))
    reps = width // LANES
    cos = cos_ref[...]
    sin = sin_ref[...]
    if reps > 1:
        cos = jnp.concatenate([cos] * reps, axis=-1)
        sin = jnp.concatenate([sin] * reps, axis=-1)
    return x * cos + partner * sin


def _mla_prep_kernel(ckv_x_ref, sm_x_ref, ckv_c_ref, sm_c_ref, qn_ref, qr_ref,
                     wuk_ref, wuv_ref, kvg_ref, kg_ref, qg_ref, qgr_ref, cos_ref, sin_ref,
                     k_ref, v_ref, q_ref):
    i = pl.program_id(1)
    kg_n = kg_ref[:, :A_NOPE]
    kg_r = kg_ref[:, A_NOPE:]

    def keys_values(ckv_ref, sm_ref, rotate):
        ckv = ckv_ref[...].astype(F32)
        cn = (ckv * lax.rsqrt(jnp.mean(ckv * ckv, axis=-1, keepdims=True) + EPS) * kvg_ref[...]).astype(BF16)
        v_ref[...] = jnp.dot(cn, wuv_ref[...], preferred_element_type=F32).astype(v_ref.dtype)
        kn = jnp.dot(cn, wuk_ref[...], preferred_element_type=F32)
        kr = sm_ref[:, S_KR:S_KR + A_ROPE]
        kr_ss = jnp.sum(kr * kr, axis=-1, keepdims=True)
        krg = kr * kg_r
        if rotate:
            pad = jnp.concatenate([krg, jnp.zeros_like(krg)], axis=-1)
            krg = _rope(pad, cos_ref, sin_ref)[:, :A_ROPE]
        for hd in range(A_HEADS):
            kh = kn[:, hd * A_NOPE:(hd + 1) * A_NOPE]
            r = lax.rsqrt((jnp.sum(kh * kh, axis=-1, keepdims=True) + kr_ss) * (1.0 / A_QK) + EPS)
            k_ref[hd, :, :A_NOPE] = (kh * r * kg_n).astype(k_ref.dtype)
            k_ref[hd, :, A_NOPE:] = (krg * r).astype(k_ref.dtype)

    @pl.when(i == 0)
    def _():
        keys_values(ckv_c_ref, sm_c_ref, False)

    @pl.when(i > 0)
    def _():
        keys_values(ckv_x_ref, sm_x_ref, True)
        qn = qn_ref[...].astype(F32)
        qr = qr_ref[...].astype(F32)
        qr_rot = _rope(qr * qgr_ref[...], cos_ref, sin_ref)
        qr_sq = qr * qr
        scale = A_QK ** -0.5
        for hd in range(A_HEADS):
            qh = qn[:, hd * A_NOPE:(hd + 1) * A_NOPE]
            rs = slice(hd * A_ROPE, (hd + 1) * A_ROPE)
            ss = jnp.sum(qh * qh, axis=-1, keepdims=True) + jnp.sum(qr_sq[:, rs], axis=-1, keepdims=True)
            r = lax.rsqrt(ss * (1.0 / A_QK) + EPS) * scale
            q_ref[hd, :, :A_NOPE] = (qh * r * qg_ref[:, :A_NOPE]).astype(q_ref.dtype)
            q_ref[hd, :, A_NOPE:] = (qr_rot[:, rs] * r).astype(q_ref.dtype)


def _mla_prep(proj, small, proj_c, small_c, w_uk, w_uv, kv_norm_g, k_norm_g, q_norm_g, cos_t, sin_t):
    b, t, _ = proj.shape
    tc = proj_c.shape[1]
    tr = tc
    nx = t // tr
    xrow = lambda bi, i: (bi, jnp.maximum(i - 1, 0), 0)
    qg_rope = jnp.tile(q_norm_g[A_NOPE:], A_HEADS).reshape(1, A_HEADS * A_ROPE)
    return pl.pallas_call(
        _mla_prep_kernel,
        grid=(b, nx + 1),
        in_specs=[pl.BlockSpec((None, tr, KV_RANK), lambda bi, i: (bi, jnp.maximum(i - 1, 0), P_CKV // KV_RANK)),
                  pl.BlockSpec((None, tr, S_COLS), xrow),
                  pl.BlockSpec((None, tr, KV_RANK), lambda bi, i: (bi, 0, C_CKV // KV_RANK)),
                  pl.BlockSpec((None, tr, S_COLS), lambda bi, i: (bi, 0, 0)),
                  pl.BlockSpec((None, tr, A_HEADS * A_NOPE), lambda bi, i: (bi, jnp.maximum(i - 1, 0), P_QAN // (A_HEADS * A_NOPE))),
                  pl.BlockSpec((None, tr, A_HEADS * A_ROPE), lambda bi, i: (bi, jnp.maximum(i - 1, 0), P_QAR // (A_HEADS * A_ROPE))),
                  pl.BlockSpec((KV_RANK, A_HEADS * A_NOPE), lambda bi, i: (0, 0)),
                  pl.BlockSpec((KV_RANK, A_V_W), lambda bi, i: (0, 0)),
                  pl.BlockSpec((1, KV_RANK), lambda bi, i: (0, 0)),
                  pl.BlockSpec((1, A_QK), lambda bi, i: (0, 0)),
                  pl.BlockSpec((1, A_QK), lambda bi, i: (0, 0)),
                  pl.BlockSpec((1, A_HEADS * A_ROPE), lambda bi, i: (0, 0)),
                  pl.BlockSpec((tr, LANES), lambda bi, i: (jnp.maximum(i - 1, 0), 0)),
                  pl.BlockSpec((tr, LANES), lambda bi, i: (jnp.maximum(i - 1, 0), 0))],
        out_specs=[pl.BlockSpec((None, A_HEADS, tr, A_QK), lambda bi, i: (bi, 0, i, 0)),
                   pl.BlockSpec((None, tr, A_V_W), lambda bi, i: (bi, i, 0)),
                   pl.BlockSpec((None, A_HEADS, tr, A_QK), lambda bi, i: (bi, 0, jnp.maximum(i - 1, 0), 0))],
        out_shape=[jax.ShapeDtypeStruct((b, A_HEADS, tc + t, A_QK), BF16),
                   jax.ShapeDtypeStruct((b, tc + t, A_V_W), BF16),
                   jax.ShapeDtypeStruct((b, A_HEADS, t, A_QK), BF16)],
        compiler_params=_cparams(("parallel", "arbitrary")),
        name="mla_prep",
    )(proj, small, proj_c, small_c, proj, proj, w_uk, w_uv, kv_norm_g.reshape(1, KV_RANK),
      k_norm_g.reshape(1, A_QK), q_norm_g.reshape(1, A_QK), qg_rope, cos_t, sin_t)


def _attn_kernel(q_ref, k_ref, v_ref, o_ref):
    s = lax.dot_general(q_ref[...], k_ref[...], (((1,), (1,)), ((), ())), preferred_element_type=F32)
    p = jnp.exp(s - jnp.max(s, axis=-1, keepdims=True))
    l = jnp.sum(p, axis=-1, keepdims=True)
    o = jnp.dot(p.astype(BF16), v_ref[...], preferred_element_type=F32)
    o_ref[...] = (o / l).astype(o_ref.dtype)


def _attention(q, k, v):
    b, h, s, _ = q.shape
    tk = k.shape[2]
    tq = min(512, s)
    return pl.pallas_call(
        _attn_kernel,
        grid=(b, h, s // tq),
        in_specs=[pl.BlockSpec((None, None, tq, A_QK), lambda bi, hi, i: (bi, hi, i, 0)),
                  pl.BlockSpec((None, None, tk, A_QK), lambda bi, hi, i: (bi, hi, 0, 0)),
                  pl.BlockSpec((None, tk, A_DV), lambda bi, hi, i: (bi, 0, hi))],
        out_specs=pl.BlockSpec((None, tq, A_DV), lambda bi, hi, i: (bi, i, hi)),
        out_shape=jax.ShapeDtypeStruct((b, s, A_V_W), BF16),
        compiler_params=_cparams(("parallel", "parallel", "parallel")),
        name="attention",
    )(q, k, v)


def _merge_kernel(hf_ref, hb_ref, om_ref, zm_ref, oa_ref, za_ref, gm_ref, ga_ref, mhg_ref,
                  wm_ref, wa_ref, o_ref, hm_ref):
    for hd in range(M_HEADS):
        vs = slice(hd * M_DV, (hd + 1) * M_DV)
        h = hf_ref[:, vs].astype(F32) + hb_ref[:, vs].astype(F32)
        hn = h * lax.rsqrt(jnp.mean(h * h, axis=-1, keepdims=True) + EPS) * mhg_ref[:, vs]
        gated = hn * jax.nn.sigmoid(om_ref[:, vs].astype(F32)) * _silu(zm_ref[:, vs].astype(F32))
        hm_ref[:, vs] = gated.astype(BF16)
    p_m = jnp.dot(hm_ref[...], wm_ref[...], preferred_element_type=F32)
    oa = (oa_ref[...].astype(F32) * _silu(za_ref[...].astype(F32))).astype(BF16)
    p_a = jnp.dot(oa, wa_ref[...], preferred_element_type=F32)
    y = jax.nn.sigmoid(gm_ref[...].astype(F32)) * p_m + jax.nn.sigmoid(ga_ref[...].astype(F32)) * p_a
    o_ref[...] = y.astype(o_ref.dtype)


def _merge(hf, hb, proj, oa, mh_norm_g, w_proj_m, w_proj_a):
    b, t, _ = hf.shape
    tm = 256
    row = lambda bi, i: (bi, i, 0)
    col = lambda c: (lambda bi, i: (bi, i, c))
    wspec = pl.BlockSpec((D_MODEL, D_MODEL), lambda bi, i: (0, 0))
    act = lambda im: pl.BlockSpec((None, tm, D_MODEL), im)
    return pl.pallas_call(
        _merge_kernel,
        grid=(b, t // tm),
        in_specs=[act(row), act(row),
                  act(col(P_OM // D_MODEL)), act(col(P_ZM // D_MODEL)),
                  act(row), act(col(P_ZA // D_MODEL)),
                  act(col(P_GM // D_MODEL)), act(col(P_GM // D_MODEL + 1)),
                  pl.BlockSpec((1, M_V_W), lambda bi, i: (0, 0)),
                  wspec, wspec],
        out_specs=act(row),
        out_shape=jax.ShapeDtypeStruct((b, t, D_MODEL), BF16),
        scratch_shapes=[pltpu.VMEM((tm, M_V_W), BF16)],
        compiler_params=_cparams(("parallel", "parallel")),
        name="merge",
    )(hf, hb, proj, proj, oa, proj, proj, proj, mh_norm_g.reshape(1, M_V_W), w_proj_m, w_proj_a)


def _out_kernel(y_ref, x_ref, gate_ref, w_ref, o_ref):
    o_ref[...] = x_ref[...] + gate_ref[...] * jnp.dot(y_ref[...], w_ref[...], preferred_element_type=F32)


def _out(y, x, gate, w_out):
    b, t, _ = x.shape
    tm = min(512, t)
    row = lambda bi, i: (bi, i, 0)
    return pl.pallas_call(
        _out_kernel,
        grid=(b, t // tm),
        in_specs=[pl.BlockSpec((None, tm, D_MODEL), row),
                  pl.BlockSpec((None, tm, D_MODEL), row),
                  pl.BlockSpec((None, 1, D_MODEL), lambda bi, i: (bi, 0, 0)),
                  pl.BlockSpec((D_MODEL, D_MODEL), lambda bi, i: (0, 0))],
        out_specs=pl.BlockSpec((None, tm, D_MODEL), row),
        out_shape=jax.ShapeDtypeStruct((b, t, D_MODEL), F32),
        compiler_params=_cparams(("parallel", "parallel")),
        name="out",
    )(y, x, gate, w_out)


def _layout_w_in(w):
    qa = w[:, _O_QA:_O_ZA].reshape(D_MODEL, A_HEADS, A_QK)
    qa_n = qa[:, :, :A_NOPE].reshape(D_MODEL, A_HEADS * A_NOPE)
    qa_r = qa[:, :, A_NOPE:].reshape(D_MODEL, A_HEADS * A_ROPE)
    km, vm, ckv = w[:, _O_KM:_O_VM], w[:, _O_VM:_O_GT], w[:, _O_CKV:_O_KR]
    main = jnp.concatenate([vm, w[:, _O_OM:_O_ZM], w[:, _O_ZM:_O_QA], w[:, _O_ZA:_O_GM], w[:, _O_GM:_O_END],
                            qa_n, km, w[:, _O_QM:_O_OM], qa_r, ckv], axis=1).astype(BF16)
    ctx = jnp.concatenate([vm, km, ckv], axis=1).astype(BF16)
    small = jnp.concatenate([w[:, _O_GT:_O_CKV], w[:, _O_KR:_O_QM],
                             jnp.zeros((D_MODEL, S_COLS - M_GATE_W - A_ROPE), w.dtype)], axis=1).astype(BF16)
    return main, ctx, small


def _rope_tables(seq):
    pos = np.arange(seq)
    lane = np.arange(LANES) % A_ROPE
    axis = lane // (2 * ROPE_FREQS)
    half = (lane % (2 * ROPE_FREQS)) // ROPE_FREQS
    freqs = jnp.asarray(ROPE_THETA, F32) ** (-jnp.arange(ROPE_FREQS, dtype=F32) / ROPE_FREQS)
    coord = jnp.where(jnp.asarray(axis == 0)[None, :],
                      jnp.asarray(pos // GRID_W, F32)[:, None], jnp.asarray(pos % GRID_W, F32)[:, None])
    ang = coord * freqs[jnp.asarray(lane % ROPE_FREQS)][None, :]
    sign = jnp.asarray(np.where(half == 0, -1.0, 1.0), F32)[None, :]
    return jnp.cos(ang), jnp.sin(ang) * sign


def _layer(x, c, ctx, c_ctx, ada_w, ada_b, norm_g, w_in, conv_w, conv_b, gate_b, mh_norm_g, q_norm_g, k_norm_g,
           kv_norm_g, w_uk, w_uv, w_proj_m, w_proj_a, w_out):
    b, t, _ = x.shape
    tc = ctx.shape[1]
    assert tc == MLSTM_CHUNK and t % MLSTM_CHUNK == 0 and t % GRID_W == 0

    c8 = jnp.zeros((8, D_MODEL), F32).at[:b].set(c).at[b].set(c_ctx)
    mod = _adaln(c8, ada_w, ada_b)
    shift, scale, gate = mod[:, :D_MODEL], mod[:, D_MODEL:2 * D_MODEL], mod[:, 2 * D_MODEL:]
    per_b = lambda a: a[:b].reshape(b, 1, D_MODEL)
    per_c = lambda a: jnp.broadcast_to(a[b].reshape(1, 1, D_MODEL), (b, 1, D_MODEL))

    w_main, w_ctx, w_small = _layout_w_in(w_in)
    proj, small = _proj(x, per_b(scale), per_b(shift), norm_g, w_main, w_small, P_COLS, 1280)
    proj_c, small_c = _proj(ctx, per_c(scale), per_c(shift), norm_g, w_ctx, w_small, C_COLS, C_COLS // 2)

    cw_q, cw_k = conv_w[:, :M_QK_W], conv_w[:, M_QK_W:]
    cb_q, cb_k = conv_b[:M_QK_W], conv_b[M_QK_W:]
    (q_m,) = _conv(proj, P_QM, cw_q, cb_q, M_DQK ** -0.5, False)
    k_m, kt_m = _conv(proj, P_KM, cw_k, cb_k, 1.0, True)
    k_c, kt_c = _conv(proj_c, C_KM, cw_k, cb_k, 1.0, True)
    gate_bias = jnp.zeros((1, S_COLS), F32).at[0, :M_GATE_W].set(gate_b)
    hf, hb = _mlstm(q_m, k_m, kt_m, proj, small, k_c, kt_c, proj_c, small_c, gate_bias)

    cos_t, sin_t = _rope_tables(t)
    k_a, v_a, q_a = _mla_prep(proj, small, proj_c, small_c, w_uk.astype(BF16), w_uv.astype(BF16),
                              kv_norm_g, k_norm_g, q_norm_g, cos_t, sin_t)
    o_a = _attention(q_a, k_a, v_a)

    y = _merge(hf, hb, proj, o_a, mh_norm_g, w_proj_m.astype(BF16), w_proj_a.astype(BF16))
    return _out(y, x, per_b(gate), w_out.astype(BF16))


def kernel(x, c, ctx, c_ctx, ada_w, ada_b, norm_g, w_in, conv_w, conv_b, gate_b, mh_norm_g, q_norm_g, k_norm_g,
           kv_norm_g, w_uk, w_uv, w_proj_m, w_proj_a, w_out):
    assert ada_w.shape[0] == 1, "single-layer block"
    return _layer(x, c, ctx, c_ctx, ada_w[0], ada_b[0], norm_g[0], w_in[0], conv_w[0], conv_b[0], gate_b[0],
                  mh_norm_g[0], q_norm_g[0], k_norm_g[0], kv_norm_g[0], w_uk[0], w_uv[0], w_proj_m[0],
                  w_proj_a[0], w_out[0])
```

```python
import functools

import numpy as np
import jax
import jax.numpy as jnp
from jax import lax
from jax.experimental import pallas as pl
from jax.experimental.pallas import tpu as pltpu

F32 = jnp.float32
BF16 = jnp.bfloat16

D_MODEL = 2048
GRID_W = 64
EPS = 1e-6

M_HEADS = 8
M_DQK = 128
M_DV = 256
M_CONV = 5
A_HEADS = 16
A_NOPE = 128
A_ROPE = 64
A_QK = A_NOPE + A_ROPE
A_DV = 128
A_VX = 256
KV_RANK = 512
ROPE_FREQS = A_ROPE // 4
ROPE_THETA = 10000.0

M_QK_W = M_HEADS * M_DQK
M_V_W = M_HEADS * M_DV
M_GATE_W = 4 * M_HEADS
A_Q_W = A_HEADS * A_QK
A_V_W = A_HEADS * A_DV

_O_KM = 0
_O_VM = _O_KM + M_QK_W
_O_GT = _O_VM + M_V_W
_O_CKV = _O_GT + M_GATE_W
_O_KR = _O_CKV + KV_RANK
_O_QM = _O_KR + A_ROPE
_O_OM = _O_QM + M_QK_W
_O_ZM = _O_OM + M_V_W
_O_QA = _O_ZM + M_V_W
_O_ZA = _O_QA + A_Q_W
_O_GM = _O_ZA + A_V_W
_O_END = _O_GM + 2 * D_MODEL

P_VM = 0
P_OM = P_VM + M_V_W
P_ZM = P_OM + M_V_W
P_ZA = P_ZM + M_V_W
P_GM = P_ZA + A_V_W
P_QAN = P_GM + 2 * D_MODEL
P_KM = P_QAN + A_HEADS * A_NOPE
P_QM = P_KM + M_QK_W
P_QAR = P_QM + M_QK_W
P_CKV = P_QAR + A_HEADS * A_ROPE
P_COLS = P_CKV + KV_RANK
C_VM = 0
C_KM = C_VM + M_V_W
C_CKV = C_KM + M_QK_W
C_COLS = C_CKV + KV_RANK
S_GT = 0
S_KR = M_GATE_W
S_COLS = 128

LANES = 128
MLSTM_CHUNK = 256
NEG_BIG = -1e30
LOG2_E = 1.4426950408889634
VMEM_LIMIT = 60 * 1024 * 1024


def _cparams(sem):
    return pltpu.CompilerParams(dimension_semantics=sem, vmem_limit_bytes=VMEM_LIMIT)


def _silu(a):
    return a * jax.nn.sigmoid(a)


def _adaln_kernel(c_ref, w_ref, b_ref, o_ref):
    s = _silu(c_ref[...])
    o_ref[...] = jnp.dot(s.astype(BF16), w_ref[...].astype(BF16), preferred_element_type=F32) + b_ref[...]


def _adaln(c8, ada_w, ada_b):
    n = ada_w.shape[1]
    tn = 1024
    return pl.pallas_call(
        _adaln_kernel,
        grid=(n // tn,),
        in_specs=[pl.BlockSpec((8, D_MODEL), lambda j: (0, 0)),
                  pl.BlockSpec((D_MODEL, tn), lambda j: (0, j)),
                  pl.BlockSpec((1, tn), lambda j: (0, j))],
        out_specs=pl.BlockSpec((8, tn), lambda j: (0, j)),
        out_shape=jax.ShapeDtypeStruct((8, n), F32),
        compiler_params=_cparams(("parallel",)),
        name="adaln",
    )(c8, ada_w, ada_b.reshape(1, n))


def _proj_kernel(x_ref, sc_ref, sh_ref, g_ref, w_ref, ws_ref, o_ref, os_ref, h_ref):
    @pl.when(pl.program_id(2) == 0)
    def _():
        x = x_ref[...]
        ms = jnp.mean(x * x, axis=-1, keepdims=True)
        y = x * lax.rsqrt(ms + EPS) * g_ref[...]
        h = (y * (1.0 + sc_ref[...]) + sh_ref[...]).astype(BF16)
        h_ref[...] = h
        os_ref[...] = jnp.dot(h, ws_ref[...], preferred_element_type=F32)

    o_ref[...] = jnp.dot(h_ref[...], w_ref[...], preferred_element_type=F32).astype(o_ref.dtype)


def _proj(x, scale, shift, norm_g, w_main, w_small, n_cols, tn):
    b, t, _ = x.shape
    tm = min(1024, t)
    return pl.pallas_call(
        _proj_kernel,
        grid=(b, t // tm, n_cols // tn),
        in_specs=[pl.BlockSpec((None, tm, D_MODEL), lambda bi, i, j: (bi, i, 0)),
                  pl.BlockSpec((None, 1, D_MODEL), lambda bi, i, j: (bi, 0, 0)),
                  pl.BlockSpec((None, 1, D_MODEL), lambda bi, i, j: (bi, 0, 0)),
                  pl.BlockSpec((1, D_MODEL), lambda bi, i, j: (0, 0)),
                  pl.BlockSpec((D_MODEL, tn), lambda bi, i, j: (0, j)),
                  pl.BlockSpec((D_MODEL, S_COLS), lambda bi, i, j: (0, 0))],
        out_specs=[pl.BlockSpec((None, tm, tn), lambda bi, i, j: (bi, i, j)),
                   pl.BlockSpec((None, tm, S_COLS), lambda bi, i, j: (bi, i, 0))],
        out_shape=[jax.ShapeDtypeStruct((b, t, n_cols), BF16),
                   jax.ShapeDtypeStruct((b, t, S_COLS), F32)],
        scratch_shapes=[pltpu.VMEM((tm, D_MODEL), BF16)],
        compiler_params=_cparams(("parallel", "parallel", "arbitrary")),
        name="proj",
    )(x, scale, shift, norm_g.reshape(1, D_MODEL), w_main, w_small)


CONV_HALO = 16


def _conv_kernel(prev_ref, cur_ref, next_ref, w_ref, b_ref, o_ref, buf_ref, *, tt, nt, out_scale, transpose):
    i = pl.program_id(1)
    buf_ref[CONV_HALO:CONV_HALO + tt, :] = cur_ref[...].astype(F32)
    buf_ref[0:CONV_HALO, :] = prev_ref[...].astype(F32) * (i > 0).astype(F32)
    buf_ref[CONV_HALO + tt:2 * CONV_HALO + tt, :] = next_ref[...].astype(F32) * (i < nt - 1).astype(F32)
    acc = jnp.broadcast_to(b_ref[...], (tt, cur_ref.shape[-1]))
    for k in range(M_CONV):
        lo = CONV_HALO - M_CONV // 2 + k
        acc = acc + w_ref[k:k + 1, :] * buf_ref[lo:lo + tt, :]
    y = _silu(acc)
    if out_scale != 1.0:
        y = y * out_scale
    o_ref[...] = (y.T if transpose else y).astype(o_ref.dtype)


def _conv(src, col_off, w, bias, out_scale, transpose):
    b, t, _ = src.shape
    cw = 512
    tt = min(512, t)
    nt = t // tt
    cb = col_off // cw
    hb = tt // CONV_HALO
    nhalo = t // CONV_HALO
    if transpose:
        out_shape = jax.ShapeDtypeStruct((b, M_QK_W, t), BF16)
        out_specs = pl.BlockSpec((None, cw, tt), lambda bi, i, c: (bi, c, i))
    else:
        out_shape = jax.ShapeDtypeStruct((b, t, M_QK_W), BF16)
        out_specs = pl.BlockSpec((None, tt, cw), lambda bi, i, c: (bi, i, c))
    return pl.pallas_call(
        functools.partial(_conv_kernel, tt=tt, nt=nt, out_scale=out_scale, transpose=transpose),
        grid=(b, nt, M_QK_W // cw),
        in_specs=[pl.BlockSpec((None, CONV_HALO, cw), lambda bi, i, c: (bi, jnp.maximum(i * hb - 1, 0), cb + c)),
                  pl.BlockSpec((None, tt, cw), lambda bi, i, c: (bi, i, cb + c)),
                  pl.BlockSpec((None, CONV_HALO, cw), lambda bi, i, c: (bi, jnp.minimum((i + 1) * hb, nhalo - 1), cb + c)),
                  pl.BlockSpec((M_CONV, cw), lambda bi, i, c: (0, c)),
                  pl.BlockSpec((1, cw), lambda bi, i, c: (0, c))],
        out_specs=out_specs,
        out_shape=out_shape,
        scratch_shapes=[pltpu.VMEM((tt + 2 * CONV_HALO, cw), F32)],
        compiler_params=_cparams(("parallel", "parallel", "parallel")),
        name="conv_t" if transpose else "conv",
    )(src, src, src, w, bias.reshape(1, M_QK_W))


def _mlstm_unit(q, kt, v, g_col, g_row, ig_row, b_tot, mask, ones, c_ref, n_ref, m_ref, idx):
    m_prev = m_ref[idx][0:1, 0:1]
    c_prev = c_ref[idx]
    n_prev = n_ref[idx]
    a_row = ig_row - g_row
    h_out = None
    if q is not None:
        am = jnp.where(mask, a_row, NEG_BIG)
        c_col = jnp.maximum(m_prev, jnp.max(am, axis=-1, keepdims=True))
        c_b = jnp.broadcast_to(c_col, mask.shape)
        s = (jnp.dot(q, kt, preferred_element_type=F32) * jnp.exp2(am - c_b)).astype(BF16)
        w_state = jnp.exp2(m_prev - c_b)
        num = (jnp.dot(s, v, preferred_element_type=F32)
               + w_state * jnp.dot(q, c_prev.astype(BF16), preferred_element_type=F32))
        den = (jnp.dot(s, ones, preferred_element_type=F32)
               + w_state * jnp.dot(q, n_prev.astype(BF16), preferred_element_type=F32))
        h_out = num / jnp.maximum(jnp.abs(den), jnp.exp2(-(g_col + c_col)))
    w_row = b_tot + a_row
    m_new = jnp.maximum(b_tot + m_prev, jnp.max(w_row, axis=-1, keepdims=True))
    decay = jnp.exp2(b_tot + m_prev - m_new)
    kw = kt * jnp.exp2(w_row - m_new).astype(BF16)
    c_ref[idx] = decay * c_prev + jnp.dot(kw, v, preferred_element_type=F32)
    n_ref[idx] = decay * n_prev + jnp.dot(kw, ones, preferred_element_type=F32)
    m_ref[idx] = jnp.broadcast_to(m_new, m_ref.shape[1:])
    return h_out


def _mlstm_gates(gt_ref, gb_ref, tri):
    a = gt_ref[...] + gb_ref[...]
    lane = lax.broadcasted_iota(jnp.int32, a.shape, 1)
    is_forget = ((lane // M_HEADS) % 2) == 1
    act = jnp.where(is_forget, jax.nn.log_sigmoid(a), a) * LOG2_E
    cum = jnp.dot(tri, act, preferred_element_type=F32, precision=lax.Precision.HIGHEST)
    return cum, act.T, cum.T


def _mlstm_kernel(qf_ref, ktf_ref, vf_ref, gf_ref,
                  qb_ref, ktb_ref, vb_ref, gb_ref,
                  ktc_ref, vc_ref, gc_ref, gbias_ref,
                  hf_ref, hb_ref, c_ref, n_ref, m_ref):
    i = pl.program_id(1)
    L = MLSTM_CHUNK
    rows = lax.broadcasted_iota(jnp.int32, (L, L), 0)
    cols = lax.broadcasted_iota(jnp.int32, (L, L), 1)
    lower = cols <= rows
    upper = cols >= rows
    ones = jnp.ones((L, M_DV), BF16)

    def run(q_ref, kt_ref, v_ref, g_ref, h_ref, direction):
        causal = direction == 0
        mask = lower if causal else upper
        cum, act_t, cum_t = _mlstm_gates(g_ref, gbias_ref, mask.astype(F32))
        for hd in range(M_HEADS):
            ci = 2 * M_HEADS * direction + hd
            cf = ci + M_HEADS
            g_col = cum[:, cf:cf + 1]
            b_tot = g_col[L - 1:L, :] if causal else g_col[0:1, :]
            ks = slice(hd * M_DQK, (hd + 1) * M_DQK)
            vs = slice(hd * M_DV, (hd + 1) * M_DV)
            q = None if q_ref is None else q_ref[:, ks]
            h = _mlstm_unit(q, kt_ref[ks, :], v_ref[:, vs], g_col, cum_t[cf:cf + 1, :], act_t[ci:ci + 1, :],
                            b_tot, mask, ones, c_ref, n_ref, m_ref, direction * M_HEADS + hd)
            if h is not None:
                h_ref[:, vs] = h.astype(h_ref.dtype)

    @pl.when(i == 0)
    def _():
        c_ref[...] = jnp.zeros_like(c_ref)
        n_ref[...] = jnp.zeros_like(n_ref)
        m_ref[...] = jnp.zeros_like(m_ref)
        run(None, ktc_ref, vc_ref, gc_ref, None, 0)
        run(None, ktc_ref, vc_ref, gc_ref, None, 1)

    @pl.when(i > 0)
    def _():
        run(qf_ref, ktf_ref, vf_ref, gf_ref, hf_ref, 0)
        run(qb_ref, ktb_ref, vb_ref, gb_ref, hb_ref, 1)


def _mlstm(q, kt, proj, small, ktc, proj_c, small_c, gate_bias):
    b, t, _ = q.shape
    L = MLSTM_CHUNK
    nc = t // L
    fwd = lambda bi, i: (bi, jnp.maximum(i - 1, 0), 0)
    bwd = lambda bi, i: (bi, jnp.minimum(nc - i, nc - 1), 0)
    fwd_t = lambda bi, i: (bi, 0, jnp.maximum(i - 1, 0))
    bwd_t = lambda bi, i: (bi, 0, jnp.minimum(nc - i, nc - 1))
    ctx = lambda bi, i: (bi, 0, 0)

    def specs(rm, tm_):
        return [pl.BlockSpec((None, L, M_QK_W), rm), pl.BlockSpec((None, M_QK_W, L), tm_),
                pl.BlockSpec((None, L, M_V_W), rm), pl.BlockSpec((None, L, S_COLS), rm)]

    in_specs = (specs(fwd, fwd_t) + specs(bwd, bwd_t)
                + [pl.BlockSpec((None, M_QK_W, L), ctx), pl.BlockSpec((None, L, M_V_W), ctx),
                   pl.BlockSpec((None, L, S_COLS), ctx), pl.BlockSpec((1, S_COLS), lambda bi, i: (0, 0))])
    return pl.pallas_call(
        _mlstm_kernel,
        grid=(b, nc + 1),
        in_specs=in_specs,
        out_specs=[pl.BlockSpec((None, L, M_V_W), fwd), pl.BlockSpec((None, L, M_V_W), bwd)],
        out_shape=[jax.ShapeDtypeStruct((b, t, M_V_W), BF16)] * 2,
        scratch_shapes=[pltpu.VMEM((2 * M_HEADS, M_DQK, M_DV), F32),
                        pltpu.VMEM((2 * M_HEADS, M_DQK, M_DV), F32),
                        pltpu.VMEM((2 * M_HEADS, 8, LANES), F32)],
        compiler_params=_cparams(("parallel", "arbitrary")),
        name="mlstm",
    )(q, kt, proj, small, q, kt, proj, small, ktc, proj_c, small_c, gate_bias)


def _rope(x, cos_ref, sin_ref):
    width = x.shape[-1]
    lane = lax.broadcasted_iota(jnp.int32, x.shape, 1)
    first_half = (lane % (2 * ROPE_FREQS)) < ROPE_FREQS
    partner = jnp.where(first_half, pltpu.roll(x, width - ROPE_FREQS, 1), pltpu.roll(x, ROPE_FREQS, 1))
    reps = width // LANES
    cos = cos_ref[...]
    sin = sin_ref[...]
    if reps > 1:
        cos = jnp.concatenate([cos] * reps, axis=-1)
        sin = jnp.concatenate([sin] * reps, axis=-1)
    return x * cos + partner * sin


def _mla_prep_kernel(ckv_x_ref, sm_x_ref, ckv_c_ref, sm_c_ref, qn_ref, qr_ref,
                     wuk_ref, wuv_ref, kvg_ref, kg_ref, qg_ref, qgr_ref, cos_ref, sin_ref,
                     k_ref, v_ref, q_ref):
    i = pl.program_id(1)
    kg_n = kg_ref[:, :A_NOPE]
    kg_r = kg_ref[:, A_NOPE:]

    def keys_values(ckv_ref, sm_ref, rotate):
        ckv = ckv_ref[...].astype(F32)
        cn = (ckv * lax.rsqrt(jnp.mean(ckv * ckv, axis=-1, keepdims=True) + EPS) * kvg_ref[...]).astype(BF16)
        v = jnp.dot(cn, wuv_ref[...], preferred_element_type=F32).astype(v_ref.dtype)
        lane = lax.broadcasted_iota(jnp.int32, (v.shape[0], A_VX - A_DV), 1)
        ones_col = (lane == 0).astype(v_ref.dtype)
        for hd in range(A_HEADS):
            v_ref[:, hd * A_VX:hd * A_VX + A_DV] = v[:, hd * A_DV:(hd + 1) * A_DV]
            v_ref[:, hd * A_VX + A_DV:(hd + 1) * A_VX] = ones_col
        kn = jnp.dot(cn, wuk_ref[...], preferred_element_type=F32)
        kr = sm_ref[:, S_KR:S_KR + A_ROPE]
        kr_ss = jnp.sum(kr * kr, axis=-1, keepdims=True)
        krg = kr * kg_r
        if rotate:
            pad = jnp.concatenate([krg, jnp.zeros_like(krg)], axis=-1)
            krg = _rope(pad, cos_ref, sin_ref)[:, :A_ROPE]
        for hd in range(A_HEADS):
            kh = kn[:, hd * A_NOPE:(hd + 1) * A_NOPE]
            r = lax.rsqrt((jnp.sum(kh * kh, axis=-1, keepdims=True) + kr_ss) * (1.0 / A_QK) + EPS)
            k_ref[hd, :, :A_NOPE] = (kh * r * kg_n).astype(k_ref.dtype)
            k_ref[hd, :, A_NOPE:] = (krg * r).astype(k_ref.dtype)

    @pl.when(i == 0)
    def _():
        keys_values(ckv_c_ref, sm_c_ref, False)

    @pl.when(i > 0)
    def _():
        keys_values(ckv_x_ref, sm_x_ref, True)
        qn = qn_ref[...].astype(F32)
        qr = qr_ref[...].astype(F32)
        qr_rot = _rope(qr * qgr_ref[...], cos_ref, sin_ref)
        qr_sq = qr * qr
        scale = A_QK ** -0.5 * LOG2_E
        for hd in range(A_HEADS):
            qh = qn[:, hd * A_NOPE:(hd + 1) * A_NOPE]
            rs = slice(hd * A_ROPE, (hd + 1) * A_ROPE)
            ss = jnp.sum(qh * qh, axis=-1, keepdims=True) + jnp.sum(qr_sq[:, rs], axis=-1, keepdims=True)
            r = lax.rsqrt(ss * (1.0 / A_QK) + EPS) * scale
            q_ref[hd, :, :A_NOPE] = (qh * r * qg_ref[:, :A_NOPE]).astype(q_ref.dtype)
            q_ref[hd, :, A_NOPE:] = (qr_rot[:, rs] * r).astype(q_ref.dtype)


def _mla_prep(proj, small, proj_c, small_c, w_uk, w_uv, kv_norm_g, k_norm_g, q_norm_g, cos_t, sin_t):
    b, t, _ = proj.shape
    tc = proj_c.shape[1]
    tr = tc
    nx = t // tr
    xrow = lambda bi, i: (bi, jnp.maximum(i - 1, 0), 0)
    qg_rope = jnp.tile(q_norm_g[A_NOPE:], A_HEADS).reshape(1, A_HEADS * A_ROPE)
    return pl.pallas_call(
        _mla_prep_kernel,
        grid=(b, nx + 1),
        in_specs=[pl.BlockSpec((None, tr, KV_RANK), lambda bi, i: (bi, jnp.maximum(i - 1, 0), P_CKV // KV_RANK)),
                  pl.BlockSpec((None, tr, S_COLS), xrow),
                  pl.BlockSpec((None, tr, KV_RANK), lambda bi, i: (bi, 0, C_CKV // KV_RANK)),
                  pl.BlockSpec((None, tr, S_COLS), lambda bi, i: (bi, 0, 0)),
                  pl.BlockSpec((None, tr, A_HEADS * A_NOPE), lambda bi, i: (bi, jnp.maximum(i - 1, 0), P_QAN // (A_HEADS * A_NOPE))),
                  pl.BlockSpec((None, tr, A_HEADS * A_ROPE), lambda bi, i: (bi, jnp.maximum(i - 1, 0), P_QAR // (A_HEADS * A_ROPE))),
                  pl.BlockSpec((KV_RANK, A_HEADS * A_NOPE), lambda bi, i: (0, 0)),
                  pl.BlockSpec((KV_RANK, A_V_W), lambda bi, i: (0, 0)),
                  pl.BlockSpec((1, KV_RANK), lambda bi, i: (0, 0)),
                  pl.BlockSpec((1, A_QK), lambda bi, i: (0, 0)),
                  pl.BlockSpec((1, A_QK), lambda bi, i: (0, 0)),
                  pl.BlockSpec((1, A_HEADS * A_ROPE), lambda bi, i: (0, 0)),
                  pl.BlockSpec((tr, LANES), lambda bi, i: (jnp.maximum(i - 1, 0), 0)),
                  pl.BlockSpec((tr, LANES), lambda bi, i: (jnp.maximum(i - 1, 0), 0))],
        out_specs=[pl.BlockSpec((None, A_HEADS, tr, A_QK), lambda bi, i: (bi, 0, i, 0)),
                   pl.BlockSpec((None, tr, A_HEADS * A_VX), lambda bi, i: (bi, i, 0)),
                   pl.BlockSpec((None, A_HEADS, tr, A_QK), lambda bi, i: (bi, 0, jnp.maximum(i - 1, 0), 0))],
        out_shape=[jax.ShapeDtypeStruct((b, A_HEADS, tc + t, A_QK), BF16),
                   jax.ShapeDtypeStruct((b, tc + t, A_HEADS * A_VX), BF16),
                   jax.ShapeDtypeStruct((b, A_HEADS, t, A_QK), BF16)],
        compiler_params=_cparams(("parallel", "arbitrary")),
        name="mla_prep",
    )(proj, small, proj_c, small_c, proj, proj, w_uk, w_uv, kv_norm_g.reshape(1, KV_RANK),
      k_norm_g.reshape(1, A_QK), q_norm_g.reshape(1, A_QK), qg_rope, cos_t, sin_t)


def _attn_kernel(q_ref, k_ref, v_ref, o_ref, *, key_blocks):
    q = q_ref[...]
    m = acc = None
    for lo, hi in key_blocks:
        s = lax.dot_general(q, k_ref[lo:hi, :], (((1,), (1,)), ((), ())), preferred_element_type=F32)
        m_blk = jnp.max(s, axis=-1, keepdims=True)
        m_new = m_blk if m is None else jnp.maximum(m, m_blk)
        p = jnp.exp2(s - m_new).astype(BF16)
        pv = jnp.dot(p, v_ref[lo:hi, :], preferred_element_type=F32)
        acc = pv if acc is None else acc * jnp.exp2(m - m_new) + pv
        m = m_new
    o_ref[...] = (acc[:, :A_DV] / acc[:, A_DV:A_DV + 1]).astype(o_ref.dtype)


ATTN_KEY_BLOCK = 512


def _attention(q, k, v, n_ctx):
    b, h, s, _ = q.shape
    tk = k.shape[2]
    tq = min(1024, s)
    key_blocks =[(0, n_ctx)] + [(lo, lo + ATTN_KEY_BLOCK) for lo in range(n_ctx, tk, ATTN_KEY_BLOCK)]
    return pl.pallas_call(
        functools.partial(_attn_kernel, key_blocks=tuple(key_blocks)),
        grid=(b, h, s // tq),
        in_specs=[pl.BlockSpec((None, None, tq, A_QK), lambda bi, hi, i: (bi, hi, i, 0)),
                  pl.BlockSpec((None, None, tk, A_QK), lambda bi, hi, i: (bi, hi, 0, 0)),
                  pl.BlockSpec((None, tk, A_VX), lambda bi, hi, i: (bi, 0, hi))],
        out_specs=pl.BlockSpec((None, tq, A_DV), lambda bi, hi, i: (bi, i, hi)),
        out_shape=jax.ShapeDtypeStruct((b, s, A_V_W), BF16),
        compiler_params=_cparams(("parallel", "parallel", "parallel")),
        name="attention",
    )(q, k, v)


def _merge_kernel(hf_ref, hb_ref, om_ref, zm_ref, oa_ref, za_ref, gm_ref, ga_ref, mhg_ref,
                  wm_ref, wa_ref, o_ref, hm_ref):
    for hd in range(M_HEADS):
        vs = slice(hd * M_DV, (hd + 1) * M_DV)
        h = hf_ref[:, vs].astype(F32) + hb_ref[:, vs].astype(F32)
        hn = h * lax.rsqrt(jnp.mean(h * h, axis=-1, keepdims=True) + EPS) * mhg_ref[:, vs]
        gated = hn * jax.nn.sigmoid(om_ref[:, vs].astype(F32)) * _silu(zm_ref[:, vs].astype(F32))
        hm_ref[:, vs] = gated.astype(BF16)
    p_m = jnp.dot(hm_ref[...], wm_ref[...], preferred_element_type=F32)
    oa = (oa_ref[...].astype(F32) * _silu(za_ref[...].astype(F32))).astype(BF16)
    p_a = jnp.dot(oa, wa_ref[...], preferred_element_type=F32)
    y = jax.nn.sigmoid(gm_ref[...].astype(F32)) * p_m + jax.nn.sigmoid(ga_ref[...].astype(F32)) * p_a
    o_ref[...] = y.astype(o_ref.dtype)


def _merge(hf, hb, proj, oa, mh_norm_g, w_proj_m, w_proj_a):
    b, t, _ = hf.shape
    tm = 256
    row = lambda bi, i: (bi, i, 0)
    col = lambda c: (lambda bi, i: (bi, i, c))
    wspec = pl.BlockSpec((D_MODEL, D_MODEL), lambda bi, i: (0, 0))
    act = lambda im: pl.BlockSpec((None, tm, D_MODEL), im)
    return pl.pallas_call(
        _merge_kernel,
        grid=(b, t // tm),
        in_specs=[act(row), act(row),
                  act(col(P_OM // D_MODEL)), act(col(P_ZM // D_MODEL)),
                  act(row), act(col(P_ZA // D_MODEL)),
                  act(col(P_GM // D_MODEL)), act(col(P_GM // D_MODEL + 1)),
                  pl.BlockSpec((1, M_V_W), lambda bi, i: (0, 0)),
                  wspec, wspec],
        out_specs=act(row),
        out_shape=jax.ShapeDtypeStruct((b, t, D_MODEL), BF16),
        scratch_shapes=[pltpu.VMEM((tm, M_V_W), BF16)],
        compiler_params=_cparams(("parallel", "parallel")),
        name="merge",
    )(hf, hb, proj, proj, oa, proj, proj, proj, mh_norm_g.reshape(1, M_V_W), w_proj_m, w_proj_a)


def _out_kernel(y_ref, x_ref, gate_ref, w_ref, o_ref):
    o_ref[...] = x_ref[...] + gate_ref[...] * jnp.dot(y_ref[...], w_ref[...], preferred_element_type=F32)


def _out(y, x, gate, w_out):
    b, t, _ = x.shape
    tm = min(512, t)
    row = lambda bi, i: (bi, i, 0)
    return pl.pallas_call(
        _out_kernel,
        grid=(b, t // tm),
        in_specs=[pl.BlockSpec((None, tm, D_MODEL), row),
                  pl.BlockSpec((None, tm, D_MODEL), row),
                  pl.BlockSpec((None, 1, D_MODEL), lambda bi, i: (bi, 0, 0)),
                  pl.BlockSpec((D_MODEL, D_MODEL), lambda bi, i: (0, 0))],
        out_specs=pl.BlockSpec((None, tm, D_MODEL), row),
        out_shape=jax.ShapeDtypeStruct((b, t, D_MODEL), F32),
        compiler_params=_cparams(("parallel", "parallel")),
        name="out",
    )(y, x, gate, w_out)


def _layout_w_in(w):
    qa = w[:, _O_QA:_O_ZA].reshape(D_MODEL, A_HEADS, A_QK)
    qa_n = qa[:, :, :A_NOPE].reshape(D_MODEL, A_HEADS * A_NOPE)
    qa_r = qa[:, :, A_NOPE:].reshape(D_MODEL, A_HEADS * A_ROPE)
    km, vm, ckv = w[:, _O_KM:_O_VM], w[:, _O_VM:_O_GT], w[:, _O_CKV:_O_KR]
    main = jnp.concatenate([vm, w[:, _O_OM:_O_ZM], w[:, _O_ZM:_O_QA], w[:, _O_ZA:_O_GM], w[:, _O_GM:_O_END],
                            qa_n, km, w[:, _O_QM:_O_OM], qa_r, ckv], axis=1).astype(BF16)
    ctx = jnp.concatenate([vm, km, ckv], axis=1).astype(BF16)
    small = jnp.concatenate([w[:, _O_GT:_O_CKV], w[:, _O_KR:_O_QM],
                             jnp.zeros((D_MODEL, S_COLS - M_GATE_W - A_ROPE), w.dtype)], axis=1).astype(BF16)
    return main, ctx, small


def _rope_tables(seq):
    pos = np.arange(seq)
    lane = np.arange(LANES) % A_ROPE
    axis = lane // (2 * ROPE_FREQS)
    half = (lane % (2 * ROPE_FREQS)) // ROPE_FREQS
    freqs = jnp.asarray(ROPE_THETA, F32) ** (-jnp.arange(ROPE_FREQS, dtype=F32) / ROPE_FREQS)
    coord = jnp.where(jnp.asarray(axis == 0)[None, :],
                      jnp.asarray(pos // GRID_W, F32)[:, None], jnp.asarray(pos % GRID_W, F32)[:, None])
    ang = coord * freqs[jnp.asarray(lane % ROPE_FREQS)][None, :]
    sign = jnp.asarray(np.where(half == 0, -1.0, 1.0), F32)[None, :]
    return jnp.cos(ang), jnp.sin(ang) * sign


def _layer(x, c, ctx, c_ctx, ada_w, ada_b, norm_g, w_in, conv_w, conv_b, gate_b, mh_norm_g, q_norm_g, k_norm_g,
           kv_norm_g, w_uk, w_uv, w_proj_m, w_proj_a, w_out):
    b, t, _ = x.shape
    tc = ctx.shape[1]
    assert tc == MLSTM_CHUNK and t % MLSTM_CHUNK == 0 and t % GRID_W == 0

    c8 = jnp.zeros((8, D_MODEL), F32).at[:b].set(c).at[b].set(c_ctx)
    mod = _adaln(c8, ada_w, ada_b)
    shift, scale, gate = mod[:, :D_MODEL], mod[:, D_MODEL:2 * D_MODEL], mod[:, 2 * D_MODEL:]
    per_b = lambda a: a[:b].reshape(b, 1, D_MODEL)
    per_c = lambda a: jnp.broadcast_to(a[b].reshape(1, 1, D_MODEL), (b, 1, D_MODEL))

    w_main, w_ctx, w_small = _layout_w_in(w_in)
    proj, small = _proj(x, per_b(scale), per_b(shift), norm_g, w_main, w_small, P_COLS, 1280)
    proj_c, small_c = _proj(ctx, per_c(scale), per_c(shift), norm_g, w_ctx, w_small, C_COLS, C_COLS // 2)

    cw_q, cw_k = conv_w[:, :M_QK_W], conv_w[:, M_QK_W:]
    cb_q, cb_k = conv_b[:M_QK_W], conv_b[M_QK_W:]
    q_m = _conv(proj, P_QM, cw_q, cb_q, M_DQK ** -0.5, False)
    kt_m = _conv(proj, P_KM, cw_k, cb_k, 1.0, True)
    kt_c = _conv(proj_c, C_KM, cw_k, cb_k, 1.0, True)
    gate_bias = jnp.zeros((1, S_COLS), F32).at[0, :M_GATE_W].set(gate_b)
    hf, hb = _mlstm(q_m, kt_m, proj, small, kt_c, proj_c, small_c, gate_bias)

    cos_t, sin_t = _rope_tables(t)
    k_a, v_a, q_a = _mla_prep(proj, small, proj_c, small_c, w_uk.astype(BF16), w_uv.astype(BF16),
                              kv_norm_g, k_norm_g, q_norm_g, cos_t, sin_t)
    o_a = _attention(q_a, k_a, v_a, tc)

    y = _merge(hf, hb, proj, o_a, mh_norm_g, w_proj_m.astype(BF16), w_proj_a.astype(BF16))
    return _out(y, x, per_b(gate), w_out.astype(BF16))


def kernel(x, c, ctx, c_ctx, ada_w, ada_b, norm_g, w_in, conv_w, conv_b, gate_b, mh_norm_g, q_norm_g, k_norm_g,
           kv_norm_g, w_uk, w_uv, w_proj_m, w_proj_a, w_out):
    assert ada_w.shape[0] == 1, "single-layer block"
    return _layer(x, c, ctx, c_ctx, ada_w[0], ada_b[0], norm_g[0], w_in[0], conv_w[0], conv_b[0], gate_b[0],
                  mh_norm_g[0], q_norm_g[0], k_norm_g[0], kv_norm_g[0], w_uk[0], w_uv[0], w_proj_m[0],
                  w_proj_a[0], w_out[0])
```

```python
import functools

import numpy as np
import jax
import jax.numpy as jnp
from jax import lax
from jax.experimental import pallas as pl
from jax.experimental.pallas import tpu as pltpu

F32 = jnp.float32
BF16 = jnp.bfloat16

D_MODEL = 2048
GRID_W = 64
EPS = 1e-6

M_HEADS = 8
M_DQK = 128
M_DV = 256
M_CONV = 5
A_HEADS = 16
A_NOPE = 128
A_ROPE = 64
A_QK = A_NOPE + A_ROPE
A_DV = 128
A_VX = 256
A_QKX = 256
KV_RANK = 512
ROPE_FREQS = A_ROPE // 4
ROPE_THETA = 10000.0

M_QK_W = M_HEADS * M_DQK
M_V_W = M_HEADS * M_DV
M_GATE_W = 4 * M_HEADS
A_Q_W = A_HEADS * A_QK
A_V_W = A_HEADS * A_DV

_O_KM = 0
_O_VM = _O_KM + M_QK_W
_O_GT = _O_VM + M_V_W
_O_CKV = _O_GT + M_GATE_W
_O_KR = _O_CKV + KV_RANK
_O_QM = _O_KR + A_ROPE
_O_OM = _O_QM + M_QK_W
_O_ZM = _O_OM + M_V_W
_O_QA = _O_ZM + M_V_W
_O_ZA = _O_QA + A_Q_W
_O_GM = _O_ZA + A_V_W
_O_END = _O_GM + 2 * D_MODEL

P_VM = 0
P_OM = P_VM + M_V_W
P_ZM = P_OM + M_V_W
P_ZA = P_ZM + M_V_W
P_GM = P_ZA + A_V_W
P_QA = P_GM + 2 * D_MODEL
P_KM = P_QA + A_Q_W
P_QM = P_KM + M_QK_W
P_CKV = P_QM + M_QK_W
P_COLS = P_CKV + KV_RANK
C_VM = 0
C_KM = C_VM + M_V_W
C_CKV = C_KM + M_QK_W
C_COLS = C_CKV + KV_RANK
S_GT = 0
S_KR = 128
S_COLS = 256

LANES = 128
MLSTM_CHUNK = 256
NEG_BIG = -1e30
LOG2_E = 1.4426950408889634
VMEM_LIMIT = 60 * 1024 * 1024


def _cparams(sem):
    return pltpu.CompilerParams(dimension_semantics=sem, vmem_limit_bytes=VMEM_LIMIT)


def _silu(a):
    return a * jax.nn.sigmoid(a)


def _adaln_kernel(c_ref, w_ref, b_ref, o_ref):
    s = _silu(c_ref[...])
    o_ref[...] = jnp.dot(s.astype(BF16), w_ref[...].astype(BF16), preferred_element_type=F32) + b_ref[...]


def _adaln(c8, ada_w, ada_b):
    n = ada_w.shape[1]
    tn = 1024
    return pl.pallas_call(
        _adaln_kernel,
        grid=(n // tn,),
        in_specs=[pl.BlockSpec((8, D_MODEL), lambda j: (0, 0)),
                  pl.BlockSpec((D_MODEL, tn), lambda j: (0, j)),
                  pl.BlockSpec((1, tn), lambda j: (0, j))],
        out_specs=pl.BlockSpec((8, tn), lambda j: (0, j)),
        out_shape=jax.ShapeDtypeStruct((8, n), F32),
        compiler_params=_cparams(("parallel",)),
        name="adaln",
    )(c8, ada_w, ada_b.reshape(1, n))


def _proj_kernel(x_ref, sc_ref, sh_ref, g_ref, w_ref, ws_ref, o_ref, os_ref, h_ref):
    @pl.when(pl.program_id(2) == 0)
    def _():
        x = x_ref[...]
        ms = jnp.mean(x * x, axis=-1, keepdims=True)
        y = x * lax.rsqrt(ms + EPS) * g_ref[...]
        h = (y * (1.0 + sc_ref[...]) + sh_ref[...]).astype(BF16)
        h_ref[...] = h
        os_ref[...] = jnp.dot(h, ws_ref[...], preferred_element_type=F32)

    o_ref[...] = jnp.dot(h_ref[...], w_ref[...], preferred_element_type=F32).astype(o_ref.dtype)


def _proj(x, scale, shift, norm_g, w_main, w_small, n_cols, tn):
    b, t, _ = x.shape
    tm = min(1024, t)
    return pl.pallas_call(
        _proj_kernel,
        grid=(b, t // tm, n_cols // tn),
        in_specs=[pl.BlockSpec((None, tm, D_MODEL), lambda bi, i, j: (bi, i, 0)),
                  pl.BlockSpec((None, 1, D_MODEL), lambda bi, i, j: (bi, 0, 0)),
                  pl.BlockSpec((None, 1, D_MODEL), lambda bi, i, j: (bi, 0, 0)),
                  pl.BlockSpec((1, D_MODEL), lambda bi, i, j: (0, 0)),
                  pl.BlockSpec((D_MODEL, tn), lambda bi, i, j: (0, j)),
                  pl.BlockSpec((D_MODEL, S_COLS), lambda bi, i, j: (0, 0))],
        out_specs=[pl.BlockSpec((None, tm, tn), lambda bi, i, j: (bi, i, j)),
                   pl.BlockSpec((None, tm, S_COLS), lambda bi, i, j: (bi, i, 0))],
        out_shape=[jax.ShapeDtypeStruct((b, t, n_cols), BF16),
                   jax.ShapeDtypeStruct((b, t, S_COLS), F32)],
        scratch_shapes=[pltpu.VMEM((tm, D_MODEL), BF16)],
        compiler_params=_cparams(("parallel", "parallel", "arbitrary")),
        name="proj",
    )(x, scale, shift, norm_g.reshape(1, D_MODEL), w_main, w_small)


CONV_HALO = 16


def _conv_kernel(prev_ref, cur_ref, next_ref, w_ref, b_ref, o_ref, buf_ref, *, tt, nt, out_scale, transpose):
    i = pl.program_id(1)
    buf_ref[CONV_HALO:CONV_HALO + tt, :] = cur_ref[...].astype(F32)
    buf_ref[0:CONV_HALO, :] = prev_ref[...].astype(F32) * (i > 0).astype(F32)
    buf_ref[CONV_HALO + tt:2 * CONV_HALO + tt, :] = next_ref[...].astype(F32) * (i < nt - 1).astype(F32)
    acc = jnp.broadcast_to(b_ref[...], (tt, cur_ref.shape[-1]))
    for k in range(M_CONV):
        lo = CONV_HALO - M_CONV // 2 + k
        acc = acc + w_ref[k:k + 1, :] * buf_ref[lo:lo + tt, :]
    y = _silu(acc)
    if out_scale != 1.0:
        y = y * out_scale
    o_ref[...] = (y.T if transpose else y).astype(o_ref.dtype)


def _conv(src, col_off, w, bias, out_scale, transpose):
    b, t, _ = src.shape
    cw = 512
    tt = min(512, t)
    nt = t // tt
    cb = col_off // cw
    hb = tt // CONV_HALO
    nhalo = t // CONV_HALO
    if transpose:
        out_shape = jax.ShapeDtypeStruct((b, M_QK_W, t), BF16)
        out_specs = pl.BlockSpec((None, cw, tt), lambda bi, i, c: (bi, c, i))
    else:
        out_shape = jax.ShapeDtypeStruct((b, t, M_QK_W), BF16)
        out_specs = pl.BlockSpec((None, tt, cw), lambda bi, i, c: (bi, i, c))
    return pl.pallas_call(
        functools.partial(_conv_kernel, tt=tt, nt=nt, out_scale=out_scale, transpose=transpose),
        grid=(b, nt, M_QK_W // cw),
        in_specs=[pl.BlockSpec((None, CONV_HALO, cw), lambda bi, i, c: (bi, jnp.maximum(i * hb - 1, 0), cb + c)),
                  pl.BlockSpec((None, tt, cw), lambda bi, i, c: (bi, i, cb + c)),
                  pl.BlockSpec((None, CONV_HALO, cw), lambda bi, i, c: (bi, jnp.minimum((i + 1) * hb, nhalo - 1), cb + c)),
                  pl.BlockSpec((M_CONV, cw), lambda bi, i, c: (0, c)),
                  pl.BlockSpec((1, cw), lambda bi, i, c: (0, c))],
        out_specs=out_specs,
        out_shape=out_shape,
        scratch_shapes=[pltpu.VMEM((tt + 2 * CONV_HALO, cw), F32)],
        compiler_params=_cparams(("parallel", "parallel", "parallel")),
        name="conv_t" if transpose else "conv",
    )(src, src, src, w, bias.reshape(1, M_QK_W))


def _mlstm_unit(q, kt, v, g_col, g_row, ig_row, b_tot, mask, ones, c_ref, n_ref, m_ref, idx):
    m_prev = m_ref[idx][0:1, 0:1]
    c_prev = c_ref[idx]
    n_prev = n_ref[idx]
    a_row = ig_row - g_row
    h_out = None
    if q is not None:
        am = jnp.where(mask, a_row, NEG_BIG)
        c_col = jnp.maximum(m_prev, jnp.max(am, axis=-1, keepdims=True))
        c_b = jnp.broadcast_to(c_col, mask.shape)
        s = (jnp.dot(q, kt, preferred_element_type=F32) * jnp.exp2(am - c_b)).astype(BF16)
        w_state = jnp.exp2(m_prev - c_b)
        num = (jnp.dot(s, v, preferred_element_type=F32)
               + w_state * jnp.dot(q, c_prev.astype(BF16), preferred_element_type=F32))
        den = (jnp.dot(s, ones, preferred_element_type=F32)
               + w_state[:, :LANES] * jnp.dot(q, n_prev.astype(BF16), preferred_element_type=F32))
        den = jnp.maximum(jnp.abs(den), jnp.exp2(-(g_col + c_col)))
        h_out = num / jnp.concatenate([den] * (num.shape[-1] // LANES), axis=-1)
    w_row = b_tot + a_row
    m_new = jnp.maximum(b_tot + m_prev, jnp.max(w_row, axis=-1, keepdims=True))
    decay = jnp.exp2(b_tot + m_prev - m_new)
    kw = kt * jnp.exp2(w_row - m_new).astype(BF16)
    c_ref[idx] = decay * c_prev + jnp.dot(kw, v, preferred_element_type=F32)
    n_ref[idx] = decay * n_prev + jnp.dot(kw, ones, preferred_element_type=F32)
    m_ref[idx] = jnp.broadcast_to(m_new, m_ref.shape[1:])
    return h_out


def _mlstm_gates(gt_ref, gb_ref, tri):
    a = gt_ref[...] + gb_ref[...]
    lane = lax.broadcasted_iota(jnp.int32, a.shape, 1)
    is_forget = ((lane // M_HEADS) % 2) == 1
    act = jnp.where(is_forget, jax.nn.log_sigmoid(a), a) * LOG2_E
    cum = jnp.dot(tri, act, preferred_element_type=F32, precision=lax.Precision.HIGHEST)
    return cum, act.T, cum.T


def _mlstm_kernel(qf_ref, ktf_ref, vf_ref, gf_ref,
                  qb_ref, ktb_ref, vb_ref, gb_ref,
                  ktc_ref, vc_ref, gc_ref, gbias_ref,
                  hf_ref, hb_ref, c_ref, n_ref, m_ref):
    i = pl.program_id(1)
    L = MLSTM_CHUNK
    rows = lax.broadcasted_iota(jnp.int32, (L, L), 0)
    cols = lax.broadcasted_iota(jnp.int32, (L, L), 1)
    lower = cols <= rows
    upper = cols >= rows
    ones = jnp.ones((L, LANES), BF16)

    def run(q_ref, kt_ref, v_ref, g_ref, h_ref, direction):
        causal = direction == 0
        mask = lower if causal else upper
        cum, act_t, cum_t = _mlstm_gates(g_ref, gbias_ref, mask.astype(F32))
        for hd in range(M_HEADS):
            ci = 2 * M_HEADS * direction + hd
            cf = ci + M_HEADS
            g_col = cum[:, cf:cf + 1]
            b_tot = g_col[L - 1:L, :] if causal else g_col[0:1, :]
            ks = slice(hd * M_DQK, (hd + 1) * M_DQK)
            vs = slice(hd * M_DV, (hd + 1) * M_DV)
            q = None if q_ref is None else q_ref[:, ks]
            h = _mlstm_unit(q, kt_ref[ks, :], v_ref[:, vs], g_col, cum_t[cf:cf + 1, :], act_t[ci:ci + 1, :],
                            b_tot, mask, ones, c_ref, n_ref, m_ref, direction * M_HEADS + hd)
            if h is not None:
                h_ref[:, vs] = h.astype(h_ref.dtype)

    @pl.when(i == 0)
    def _():
        c_ref[...] = jnp.zeros_like(c_ref)
        n_ref[...] = jnp.zeros_like(n_ref)
        m_ref[...] = jnp.zeros_like(m_ref)
        run(None, ktc_ref, vc_ref, gc_ref, None, 0)
        run(None, ktc_ref, vc_ref, gc_ref, None, 1)

    @pl.when(i > 0)
    def _():
        run(qf_ref, ktf_ref, vf_ref, gf_ref, hf_ref, 0)
        run(qb_ref, ktb_ref, vb_ref, gb_ref, hb_ref, 1)


def _mlstm(q, kt, proj, small, ktc, proj_c, small_c, gate_bias):
    b, t, _ = q.shape
    L = MLSTM_CHUNK
    nc = t // L
    fwd = lambda bi, i: (bi, jnp.maximum(i - 1, 0), 0)
    bwd = lambda bi, i: (bi, jnp.minimum(nc - i, nc - 1), 0)
    fwd_t = lambda bi, i: (bi, 0, jnp.maximum(i - 1, 0))
    bwd_t = lambda bi, i: (bi, 0, jnp.minimum(nc - i, nc - 1))
    ctx = lambda bi, i: (bi, 0, 0)

    def specs(rm, tm_):
        return [pl.BlockSpec((None, L, M_QK_W), rm), pl.BlockSpec((None, M_QK_W, L), tm_),
                pl.BlockSpec((None, L, M_V_W), rm), pl.BlockSpec((None, L, LANES), rm)]

    in_specs = (specs(fwd, fwd_t) + specs(bwd, bwd_t)
                + [pl.BlockSpec((None, M_QK_W, L), ctx), pl.BlockSpec((None, L, M_V_W), ctx),
                   pl.BlockSpec((None, L, LANES), ctx), pl.BlockSpec((1, LANES), lambda bi, i: (0, 0))])
    return pl.pallas_call(
        _mlstm_kernel,
        grid=(b, nc + 1),
        in_specs=in_specs,
        out_specs=[pl.BlockSpec((None, L, M_V_W), fwd), pl.BlockSpec((None, L, M_V_W), bwd)],
        out_shape=[jax.ShapeDtypeStruct((b, t, M_V_W), BF16)] * 2,
        scratch_shapes=[pltpu.VMEM((2 * M_HEADS, M_DQK, M_DV), F32),
                        pltpu.VMEM((2 * M_HEADS, M_DQK, LANES), F32),
                        pltpu.VMEM((2 * M_HEADS, 8, LANES), F32)],
        compiler_params=_cparams(("parallel", "arbitrary")),
        name="mlstm",
    )(q, kt, proj, small, q, kt, proj, small, ktc, proj_c, small_c, gate_bias)


def _rope_partner_index():
    lane = np.arange(LANES)
    return np.where(lane % (2 * ROPE_FREQS) < ROPE_FREQS, lane + ROPE_FREQS, lane - ROPE_FREQS)


def _mla_prep_kernel(ckv_x_ref, sm_x_ref, ckv_c_ref, sm_c_ref, qa_ref,
                     wuk_ref, wuv_ref, kvg_ref, kgn_ref, kgr_ref, qgw_ref, qgs_ref, cos_ref, sin_ref,
                     pair_ref, sel_ref, half_ref, swap_ref,
                     k_ref, v_ref, q_ref):
    i = pl.program_id(1)
    rows = ckv_x_ref.shape[0]
    lane_id = lax.broadcasted_iota(jnp.int32, (rows, LANES), 1)
    low_half = lane_id < A_ROPE
    low_half_of_pair = (lane_id % (2 * ROPE_FREQS)) < ROPE_FREQS
    inv_dim = 1.0 / A_QK

    def keys_values(ckv_ref, sm_ref, rotate):
        ckv = ckv_ref[...].astype(F32)
        cn = (ckv * lax.rsqrt(jnp.mean(ckv * ckv, axis=-1, keepdims=True) + EPS) * kvg_ref[...]).astype(BF16)
        v = jnp.dot(cn, wuv_ref[...], preferred_element_type=F32).astype(v_ref.dtype)
        lane = lax.broadcasted_iota(jnp.int32, (rows, A_VX - A_DV), 1)
        ones_col = (lane == 0).astype(v_ref.dtype)
        for hd in range(A_HEADS):
            v_ref[:, hd * A_VX:hd * A_VX + A_DV] = v[:, hd * A_DV:(hd + 1) * A_DV]
            v_ref[:, hd * A_VX + A_DV:(hd + 1) * A_VX] = ones_col
        kn = jnp.dot(cn, wuk_ref[...], preferred_element_type=F32)
        sq = (kn * kn).astype(BF16)
        pair = pair_ref[...]
        ss = jnp.concatenate([jnp.dot(sq[:, g * 2 * A_NOPE:(g + 1) * 2 * A_NOPE], pair, preferred_element_type=F32)
                              for g in range(A_HEADS // 2)], axis=-1)
        kr2 = sm_ref[:, S_KR:S_KR + LANES]
        kr_ss = jnp.dot((kr2 * kr2).astype(BF16), half_ref[...], preferred_element_type=F32)
        r = lax.rsqrt((ss + jnp.concatenate([kr_ss] * (A_HEADS // 2), axis=-1)) * inv_dim + EPS)
        kns = kn * r * kgn_ref[...]
        krg = kr2 * kgr_ref[...]
        if rotate:
            partner = jnp.where(low_half_of_pair, pltpu.roll(krg, LANES - ROPE_FREQS, 1), pltpu.roll(krg, ROPE_FREQS, 1))
            krg = krg * cos_ref[...] + partner * sin_ref[...]
        for hd in range(A_HEADS):
            hs = slice(hd * A_NOPE, (hd + 1) * A_NOPE)
            kn_h = kns[:, hs]
            kr_h = krg * r[:, hs]
            if hd % 2 == 0:
                k_ref[hd, :, :LANES] = kn_h.astype(k_ref.dtype)
                k_ref[hd, :, LANES:] = jnp.where(low_half, kr_h, 0.0).astype(k_ref.dtype)
            else:
                k_ref[hd, :, :LANES] = jnp.where(low_half, 0.0, kn_h).astype(k_ref.dtype)
                k_ref[hd, :, LANES:] = jnp.where(low_half, kn_h, kr_h).astype(k_ref.dtype)

    @pl.when(i == 0)
    def _():
        keys_values(ckv_c_ref, sm_c_ref, False)

    @pl.when(i > 0)
    def _():
        keys_values(ckv_x_ref, sm_x_ref, True)
        scale = A_QK ** -0.5 * LOG2_E
        cos_e, sin_e = cos_ref[...], sin_ref[...]
        cos_o, sin_o = jnp.where(low_half, 1.0, cos_e), jnp.where(low_half, 0.0, sin_e)
        swap = swap_ref[...]
        for hd in range(A_HEADS):
            odd = hd % 2
            lo = (hd // 2) * 3 * LANES + odd * LANES
            win = qa_ref[:, lo:lo + A_QKX]
            wf = win.astype(F32)
            ss = jnp.dot((wf * wf).astype(BF16), sel_ref[odd], preferred_element_type=F32)
            r = lax.rsqrt(ss * inv_dim + EPS) * scale
            x1 = wf[:, :LANES] * qgw_ref[odd:odd + 1, :LANES]
            x2 = wf[:, LANES:] * qgw_ref[odd:odd + 1, LANES:]
            p2 = jnp.dot(win[:, LANES:], swap, preferred_element_type=F32) * qgs_ref[odd:odd + 1, :]
            rot2 = x2 * (cos_o if odd else cos_e) + p2 * (sin_o if odd else sin_e)
            q_ref[hd, :, :LANES] = (x1 * r).astype(q_ref.dtype)
            q_ref[hd, :, LANES:] = (rot2 * r).astype(q_ref.dtype)


def _mla_prep(proj, small, proj_c, small_c, w_uk, w_uv, kv_norm_g, k_norm_g, q_norm_g, cos_t, sin_t):
    b, t, _ = proj.shape
    tc = proj_c.shape[1]
    tr = tc
    nx = t // tr
    xrow = lambda bi, i: (bi, jnp.maximum(i - 1, 0), 0)
    const = lambda *shape: pl.BlockSpec(shape, lambda bi, i: (0,) * len(shape))

    halves = lambda a: a.reshape(a.shape[:-1] + (A_HEADS // 2, 2, 2, A_NOPE // 2))
    swap_odd = lambda a: jnp.concatenate([halves(a)[..., 0:1, :, :], halves(a)[..., 1:2, ::-1, :]],
                                         axis=-3).reshape(a.shape)
    w_uk_l = swap_odd(w_uk)
    gk_n, gk_r = k_norm_g[:A_NOPE], k_norm_g[A_NOPE:]
    gq_n, gq_r = q_norm_g[:A_NOPE], q_norm_g[A_NOPE:]
    kgn = swap_odd(jnp.tile(gk_n, A_HEADS)).reshape(1, A_HEADS * A_NOPE)
    kgr = jnp.tile(gk_r, 2).reshape(1, LANES)
    zeros = jnp.zeros((A_ROPE,), F32)
    qgw = jnp.stack([jnp.concatenate([gq_n, gq_r, zeros]), jnp.concatenate([zeros, gq_n, gq_r])])
    partner = _rope_partner_index()
    qgs = qgw[:, LANES:][:, partner]

    lane = np.arange(A_QKX)
    pair = (lane[:, None] // A_NOPE == lane[None, :] // A_NOPE).astype(np.float32)
    sel = np.stack([np.broadcast_to((lane < A_QK)[:, None], (A_QKX, LANES)),
                    np.broadcast_to((lane >= A_QKX - A_QK)[:, None], (A_QKX, LANES))]).astype(np.float32)
    half = np.broadcast_to((np.arange(LANES) < A_ROPE)[:, None], (LANES, A_QKX)).astype(np.float32)
    swap = np.zeros((LANES, LANES), np.float32)
    swap[partner, np.arange(LANES)] = 1.0
    as_bf16 = lambda a: jnp.asarray(a, BF16)

    return pl.pallas_call(
        _mla_prep_kernel,
        grid=(b, nx + 1),
        in_specs=[pl.BlockSpec((None, tr, KV_RANK), lambda bi, i: (bi, jnp.maximum(i - 1, 0), P_CKV // KV_RANK)),
                  pl.BlockSpec((None, tr, S_COLS), xrow),
                  pl.BlockSpec((None, tr, KV_RANK), lambda bi, i: (bi, 0, C_CKV // KV_RANK)),
                  pl.BlockSpec((None, tr, S_COLS), lambda bi, i: (bi, 0, 0)),
                  pl.BlockSpec((None, tr, A_Q_W), lambda bi, i: (bi, jnp.maximum(i - 1, 0), P_QA // A_Q_W)),
                  const(KV_RANK, A_HEADS * A_NOPE), const(KV_RANK, A_V_W), const(1, KV_RANK),
                  const(1, A_HEADS * A_NOPE), const(1, LANES), const(2, A_QKX), const(2, LANES),
                  pl.BlockSpec((tr, LANES), lambda bi, i: (jnp.maximum(i - 1, 0), 0)),
                  pl.BlockSpec((tr, LANES), lambda bi, i: (jnp.maximum(i - 1, 0), 0)),
                  const(A_QKX, A_QKX), const(2, A_QKX, LANES), const(LANES, A_QKX), const(LANES, LANES)],
        out_specs=[pl.BlockSpec((None, A_HEADS, tr, A_QKX), lambda bi, i: (bi, 0, i, 0)),
                   pl.BlockSpec((None, tr, A_HEADS * A_VX), lambda bi, i: (bi, i, 0)),
                   pl.BlockSpec((None, A_HEADS, tr, A_QKX), lambda bi, i: (bi, 0, jnp.maximum(i - 1, 0), 0))],
        out_shape=[jax.ShapeDtypeStruct((b, A_HEADS, tc + t, A_QKX), BF16),
                   jax.ShapeDtypeStruct((b, tc + t, A_HEADS * A_VX), BF16),
                   jax.ShapeDtypeStruct((b, A_HEADS, t, A_QKX), BF16)],
        compiler_params=_cparams(("parallel", "arbitrary")),
        name="mla_prep",
    )(proj, small, proj_c, small_c, proj, w_uk_l.astype(BF16), w_uv.astype(BF16), kv_norm_g.reshape(1, KV_RANK),
      kgn, kgr, qgw, qgs, cos_t, sin_t, as_bf16(pair), as_bf16(sel), as_bf16(half), as_bf16(swap))


def _attn_kernel(q_ref, k_ref, v_ref, o_ref, *, key_blocks):
    q = q_ref[...]
    m = acc = None
    for lo, hi in key_blocks:
        s = lax.dot_general(q, k_ref[lo:hi, :], (((1,), (1,)), ((), ())), preferred_element_type=F32)
        m_blk = jnp.max(s, axis=-1, keepdims=True)
        m_new = m_blk if m is None else jnp.maximum(m, m_blk)
        p = jnp.exp2(s - m_new).astype(BF16)
        pv = jnp.dot(p, v_ref[lo:hi, :], preferred_element_type=F32)
        acc = pv if acc is None else acc * jnp.exp2(m - m_new) + pv
        m = m_new
    o_ref[...] = (acc[:, :A_DV] / acc[:, A_DV:A_DV + 1]).astype(o_ref.dtype)


ATTN_KEY_BLOCK = 512


def _attention(q, k, v, n_ctx):
    b, h, s, _ = q.shape
    tk = k.shape[2]
    tq = min(1024, s)
    key_blocks =[(0, n_ctx)] + [(lo, lo + ATTN_KEY_BLOCK) for lo in range(n_ctx, tk, ATTN_KEY_BLOCK)]
    return pl.pallas_call(
        functools.partial(_attn_kernel, key_blocks=tuple(key_blocks)),
        grid=(b, h, s // tq),
        in_specs=[pl.BlockSpec((None, None, tq, A_QKX), lambda bi, hi, i: (bi, hi, i, 0)),
                  pl.BlockSpec((None, None, tk, A_QKX), lambda bi, hi, i: (bi, hi, 0, 0)),
                  pl.BlockSpec((None, tk, A_VX), lambda bi, hi, i: (bi, 0, hi))],
        out_specs=pl.BlockSpec((None, tq, A_DV), lambda bi, hi, i: (bi, i, hi)),
        out_shape=jax.ShapeDtypeStruct((b, s, A_V_W), BF16),
        compiler_params=_cparams(("parallel", "parallel", "parallel")),
        name="attention",
    )(q, k, v)


def _merge_kernel(hf_ref, hb_ref, om_ref, zm_ref, oa_ref, za_ref, gm_ref, ga_ref, mhg_ref,
                  wm_ref, wa_ref, o_ref, hm_ref):
    for hd in range(M_HEADS):
        vs = slice(hd * M_DV, (hd + 1) * M_DV)
        h = hf_ref[:, vs].astype(F32) + hb_ref[:, vs].astype(F32)
        hn = h * lax.rsqrt(jnp.mean(h * h, axis=-1, keepdims=True) + EPS) * mhg_ref[:, vs]
        gated = hn * jax.nn.sigmoid(om_ref[:, vs].astype(F32)) * _silu(zm_ref[:, vs].astype(F32))
        hm_ref[:, vs] = gated.astype(BF16)
    p_m = jnp.dot(hm_ref[...], wm_ref[...], preferred_element_type=F32)
    oa = (oa_ref[...].astype(F32) * _silu(za_ref[...].astype(F32))).astype(BF16)
    p_a = jnp.dot(oa, wa_ref[...], preferred_element_type=F32)
    y = jax.nn.sigmoid(gm_ref[...].astype(F32)) * p_m + jax.nn.sigmoid(ga_ref[...].astype(F32)) * p_a
    o_ref[...] = y.astype(o_ref.dtype)


def _merge(hf, hb, proj, oa, mh_norm_g, w_proj_m, w_proj_a):
    b, t, _ = hf.shape
    tm = 256
    row = lambda bi, i: (bi, i, 0)
    col = lambda c: (lambda bi, i: (bi, i, c))
    wspec = pl.BlockSpec((D_MODEL, D_MODEL), lambda bi, i: (0, 0))
    act = lambda im: pl.BlockSpec((None, tm, D_MODEL), im)
    return pl.pallas_call(
        _merge_kernel,
        grid=(b, t // tm),
        in_specs=[act(row), act(row),
                  act(col(P_OM // D_MODEL)), act(col(P_ZM // D_MODEL)),
                  act(row), act(col(P_ZA // D_MODEL)),
                  act(col(P_GM // D_MODEL)), act(col(P_GM // D_MODEL + 1)),
                  pl.BlockSpec((1, M_V_W), lambda bi, i: (0, 0)),
                  wspec, wspec],
        out_specs=act(row),
        out_shape=jax.ShapeDtypeStruct((b, t, D_MODEL), BF16),
        scratch_shapes=[pltpu.VMEM((tm, M_V_W), BF16)],
        compiler_params=_cparams(("parallel", "parallel")),
        name="merge",
    )(hf, hb, proj, proj, oa, proj, proj, proj, mh_norm_g.reshape(1, M_V_W), w_proj_m, w_proj_a)


def _out_kernel(y_ref, x_ref, gate_ref, w_ref, o_ref):
    o_ref[...] = x_ref[...] + gate_ref[...] * jnp.dot(y_ref[...], w_ref[...], preferred_element_type=F32)


def _out(y, x, gate, w_out):
    b, t, _ = x.shape
    tm = min(512, t)
    row = lambda bi, i: (bi, i, 0)
    return pl.pallas_call(
        _out_kernel,
        grid=(b, t // tm),
        in_specs=[pl.BlockSpec((None, tm, D_MODEL), row),
                  pl.BlockSpec((None, tm, D_MODEL), row),
                  pl.BlockSpec((None, 1, D_MODEL), lambda bi, i: (bi, 0, 0)),
                  pl.BlockSpec((D_MODEL, D_MODEL), lambda bi, i: (0, 0))],
        out_specs=pl.BlockSpec((None, tm, D_MODEL), row),
        out_shape=jax.ShapeDtypeStruct((b, t, D_MODEL), F32),
        compiler_params=_cparams(("parallel", "parallel")),
        name="out",
    )(y, x, gate, w_out)


def _layout_w_in(w):
    km, vm, ckv, kr = w[:, _O_KM:_O_VM], w[:, _O_VM:_O_GT], w[:, _O_CKV:_O_KR], w[:, _O_KR:_O_QM]
    main = jnp.concatenate([vm, w[:, _O_OM:_O_ZM], w[:, _O_ZM:_O_QA], w[:, _O_ZA:_O_GM], w[:, _O_GM:_O_END],
                            w[:, _O_QA:_O_ZA], km, w[:, _O_QM:_O_OM], ckv], axis=1).astype(BF16)
    ctx = jnp.concatenate([vm, km, ckv], axis=1).astype(BF16)
    small = jnp.concatenate([w[:, _O_GT:_O_CKV], jnp.zeros((D_MODEL, S_KR - M_GATE_W), w.dtype), kr, kr],
                            axis=1).astype(BF16)
    return main, ctx, small


def _rope_tables(seq):
    pos = np.arange(seq)
    lane = np.arange(LANES) % A_ROPE
    axis = lane // (2 * ROPE_FREQS)
    half = (lane % (2 * ROPE_FREQS)) // ROPE_FREQS
    freqs = jnp.asarray(ROPE_THETA, F32) ** (-jnp.arange(ROPE_FREQS, dtype=F32) / ROPE_FREQS)
    coord = jnp.where(jnp.asarray(axis == 0)[None, :],
                      jnp.asarray(pos // GRID_W, F32)[:, None], jnp.asarray(pos % GRID_W, F32)[:, None])
    ang = coord * freqs[jnp.asarray(lane % ROPE_FREQS)][None, :]
    sign = jnp.asarray(np.where(half == 0, -1.0, 1.0), F32)[None, :]
    return jnp.cos(ang), jnp.sin(ang) * sign


def _layer(x, c, ctx, c_ctx, ada_w, ada_b, norm_g, w_in, conv_w, conv_b, gate_b, mh_norm_g, q_norm_g, k_norm_g,
           kv_norm_g, w_uk, w_uv, w_proj_m, w_proj_a, w_out):
    b, t, _ = x.shape
    tc = ctx.shape[1]
    assert tc == MLSTM_CHUNK and t % MLSTM_CHUNK == 0 and t % GRID_W == 0

    c8 = jnp.zeros((8, D_MODEL), F32).at[:b].set(c).at[b].set(c_ctx)
    mod = _adaln(c8, ada_w, ada_b)
    shift, scale, gate = mod[:, :D_MODEL], mod[:, D_MODEL:2 * D_MODEL], mod[:, 2 * D_MODEL:]
    per_b = lambda a: a[:b].reshape(b, 1, D_MODEL)
    per_c = lambda a: jnp.broadcast_to(a[b].reshape(1, 1, D_MODEL), (b, 1, D_MODEL))

    w_main, w_ctx, w_small = _layout_w_in(w_in)
    proj, small = _proj(x, per_b(scale), per_b(shift), norm_g, w_main, w_small, P_COLS, 1280)
    proj_c, small_c = _proj(ctx, per_c(scale), per_c(shift), norm_g, w_ctx, w_small, C_COLS, C_COLS // 2)

    cw_q, cw_k = conv_w[:, :M_QK_W], conv_w[:, M_QK_W:]
    cb_q, cb_k = conv_b[:M_QK_W], conv_b[M_QK_W:]
    q_m = _conv(proj, P_QM, cw_q, cb_q, M_DQK ** -0.5, False)
    kt_m = _conv(proj, P_KM, cw_k, cb_k, 1.0, True)
    kt_c = _conv(proj_c, C_KM, cw_k, cb_k, 1.0, True)
    gate_bias = jnp.zeros((1, LANES), F32).at[0, :M_GATE_W].set(gate_b)
    hf, hb = _mlstm(q_m, kt_m, proj, small, kt_c, proj_c, small_c, gate_bias)

    cos_t, sin_t = _rope_tables(t)
    k_a, v_a, q_a = _mla_prep(proj, small, proj_c, small_c, w_uk, w_uv, kv_norm_g, k_norm_g, q_norm_g, cos_t, sin_t)
    o_a = _attention(q_a, k_a, v_a, tc)

    y = _merge(hf, hb, proj, o_a, mh_norm_g, w_proj_m.astype(BF16), w_proj_a.astype(BF16))
    return _out(y, x, per_b(gate), w_out.astype(BF16))


def kernel(x, c, ctx, c_ctx, ada_w, ada_b, norm_g, w_in, conv_w, conv_b, gate_b, mh_norm_g, q_norm_g, k_norm_g,
           kv_norm_g, w_uk, w_uv, w_proj_m, w_proj_a, w_out):
    assert ada_w.shape[0] == 1, "single-layer block"
    return _layer(x, c, ctx, c_ctx, ada_w[0], ada_b[0], norm_g[0], w_in[0], conv_w[0], conv_b[0], gate_b[0],
                  mh_norm_g[0], q_norm_g[0], k_norm_g[0], kv_norm_g[0], w_uk[0], w_uv[0], w_proj_m[0],
                  w_proj_a[0], w_out[0])
```

```python
import functools

import numpy as np
import jax
import jax.numpy as jnp
from jax import lax
from jax.experimental import pallas as pl
from jax.experimental.pallas import tpu as pltpu

F32 = jnp.float32
BF16 = jnp.bfloat16

D_MODEL = 2048
GRID_W = 64
EPS = 1e-6

M_HEADS = 8
M_DQK = 128
M_DV = 256
M_CONV = 5
A_HEADS = 16
A_NOPE = 128
A_ROPE = 64
A_QK = A_NOPE + A_ROPE
A_DV = 128
A_VX = 256
A_QKX = 256
KV_RANK = 512
ROPE_FREQS = A_ROPE // 4
ROPE_THETA = 10000.0

M_QK_W = M_HEADS * M_DQK
M_V_W = M_HEADS * M_DV
M_GATE_W = 4 * M_HEADS
A_Q_W = A_HEADS * A_QK
A_V_W = A_HEADS * A_DV

_O_KM = 0
_O_VM = _O_KM + M_QK_W
_O_GT = _O_VM + M_V_W
_O_CKV = _O_GT + M_GATE_W
_O_KR = _O_CKV + KV_RANK
_O_QM = _O_KR + A_ROPE
_O_OM = _O_QM + M_QK_W
_O_ZM = _O_OM + M_V_W
_O_QA = _O_ZM + M_V_W
_O_ZA = _O_QA + A_Q_W
_O_GM = _O_ZA + A_V_W
_O_END = _O_GM + 2 * D_MODEL

P_VM = 0
P_OM = P_VM + M_V_W
P_ZM = P_OM + M_V_W
P_ZA = P_ZM + M_V_W
P_GM = P_ZA + A_V_W
P_QA = P_GM + 2 * D_MODEL
P_KM = P_QA + A_Q_W
P_QM = P_KM + M_QK_W
P_CKV = P_QM + M_QK_W
P_COLS = P_CKV + KV_RANK
C_VM = 0
C_KM = C_VM + M_V_W
C_CKV = C_KM + M_QK_W
C_COLS = C_CKV + KV_RANK
S_GT = 0
S_KR = 128
S_COLS = 256

LANES = 128
MLSTM_CHUNK = 256
NEG_BIG = -1e30
LOG2_E = 1.4426950408889634
VMEM_LIMIT = 60 * 1024 * 1024


def _cparams(sem):
    return pltpu.CompilerParams(dimension_semantics=sem, vmem_limit_bytes=VMEM_LIMIT)


def _silu(a):
    return a * jax.nn.sigmoid(a)


def _adaln_kernel(c_ref, w_ref, b_ref, o_ref):
    s = _silu(c_ref[...])
    o_ref[...] = jnp.dot(s.astype(BF16), w_ref[...].astype(BF16), preferred_element_type=F32) + b_ref[...]


def _adaln(c8, ada_w, ada_b):
    n = ada_w.shape[1]
    tn = 1024
    return pl.pallas_call(
        _adaln_kernel,
        grid=(n // tn,),
        in_specs=[pl.BlockSpec((8, D_MODEL), lambda j: (0, 0)),
                  pl.BlockSpec((D_MODEL, tn), lambda j: (0, j)),
                  pl.BlockSpec((1, tn), lambda j: (0, j))],
        out_specs=pl.BlockSpec((8, tn), lambda j: (0, j)),
        out_shape=jax.ShapeDtypeStruct((8, n), F32),
        compiler_params=_cparams(("parallel",)),
        name="adaln",
    )(c8, ada_w, ada_b.reshape(1, n))


def _proj_kernel(x_ref, sc_ref, sh_ref, g_ref, w_ref, ws_ref, o_ref, os_ref, h_ref):
    @pl.when(pl.program_id(2) == 0)
    def _():
        x = x_ref[...]
        ms = jnp.mean(x * x, axis=-1, keepdims=True)
        y = x * lax.rsqrt(ms + EPS) * g_ref[...]
        h = (y * (1.0 + sc_ref[...]) + sh_ref[...]).astype(BF16)
        h_ref[...] = h
        os_ref[...] = jnp.dot(h, ws_ref[...], preferred_element_type=F32)

    o_ref[...] = jnp.dot(h_ref[...], w_ref[...], preferred_element_type=F32).astype(o_ref.dtype)


def _proj(x, scale, shift, norm_g, w_main, w_small, n_cols, tn):
    b, t, _ = x.shape
    tm = min(1024, t)
    return pl.pallas_call(
        _proj_kernel,
        grid=(b, t // tm, n_cols // tn),
        in_specs=[pl.BlockSpec((None, tm, D_MODEL), lambda bi, i, j: (bi, i, 0)),
                  pl.BlockSpec((None, 1, D_MODEL), lambda bi, i, j: (bi, 0, 0)),
                  pl.BlockSpec((None, 1, D_MODEL), lambda bi, i, j: (bi, 0, 0)),
                  pl.BlockSpec((1, D_MODEL), lambda bi, i, j: (0, 0)),
                  pl.BlockSpec((D_MODEL, tn), lambda bi, i, j: (0, j)),
                  pl.BlockSpec((D_MODEL, S_COLS), lambda bi, i, j: (0, 0))],
        out_specs=[pl.BlockSpec((None, tm, tn), lambda bi, i, j: (bi, i, j)),
                   pl.BlockSpec((None, tm, S_COLS), lambda bi, i, j: (bi, i, 0))],
        out_shape=[jax.ShapeDtypeStruct((b, t, n_cols), BF16),
                   jax.ShapeDtypeStruct((b, t, S_COLS), F32)],
        scratch_shapes=[pltpu.VMEM((tm, D_MODEL), BF16)],
        compiler_params=_cparams(("parallel", "parallel", "arbitrary")),
        name="proj",
    )(x, scale, shift, norm_g.reshape(1, D_MODEL), w_main, w_small)


CONV_HALO = 16


def _conv_kernel(prev_ref, cur_ref, next_ref, w_ref, b_ref, o_ref, buf_ref, *, tt, nt, out_scale, transpose):
    i = pl.program_id(1)
    buf_ref[CONV_HALO:CONV_HALO + tt, :] = cur_ref[...].astype(F32)
    buf_ref[0:CONV_HALO, :] = prev_ref[...].astype(F32) * (i > 0).astype(F32)
    buf_ref[CONV_HALO + tt:2 * CONV_HALO + tt, :] = next_ref[...].astype(F32) * (i < nt - 1).astype(F32)
    acc = jnp.broadcast_to(b_ref[...], (tt, cur_ref.shape[-1]))
    for k in range(M_CONV):
        lo = CONV_HALO - M_CONV // 2 + k
        acc = acc + w_ref[k:k + 1, :] * buf_ref[lo:lo + tt, :]
    y = _silu(acc)
    if out_scale != 1.0:
        y = y * out_scale
    o_ref[...] = (y.T if transpose else y).astype(o_ref.dtype)


def _conv(src, col_off, w, bias, out_scale, transpose):
    b, t, _ = src.shape
    cw = 512
    tt = min(512, t)
    nt = t // tt
    cb = col_off // cw
    hb = tt // CONV_HALO
    nhalo = t // CONV_HALO
    if transpose:
        out_shape = jax.ShapeDtypeStruct((b, M_QK_W, t), BF16)
        out_specs = pl.BlockSpec((None, cw, tt), lambda bi, i, c: (bi, c, i))
    else:
        out_shape = jax.ShapeDtypeStruct((b, t, M_QK_W), BF16)
        out_specs = pl.BlockSpec((None, tt, cw), lambda bi, i, c: (bi, i, c))
    return pl.pallas_call(
        functools.partial(_conv_kernel, tt=tt, nt=nt, out_scale=out_scale, transpose=transpose),
        grid=(b, nt, M_QK_W // cw),
        in_specs=[pl.BlockSpec((None, CONV_HALO, cw), lambda bi, i, c: (bi, jnp.maximum(i * hb - 1, 0), cb + c)),
                  pl.BlockSpec((None, tt, cw), lambda bi, i, c: (bi, i, cb + c)),
                  pl.BlockSpec((None, CONV_HALO, cw), lambda bi, i, c: (bi, jnp.minimum((i + 1) * hb, nhalo - 1), cb + c)),
                  pl.BlockSpec((M_CONV, cw), lambda bi, i, c: (0, c)),
                  pl.BlockSpec((1, cw), lambda bi, i, c: (0, c))],
        out_specs=out_specs,
        out_shape=out_shape,
        scratch_shapes=[pltpu.VMEM((tt + 2 * CONV_HALO, cw), F32)],
        compiler_params=_cparams(("parallel", "parallel", "parallel")),
        name="conv_t" if transpose else "conv",
    )(src, src, src, w, bias.reshape(1, M_QK_W))


def _mlstm_unit(q, kt, v, g_col, g_row, ig_row, b_tot, mask, ones, c_ref, n_ref, m_ref, idx):
    m_prev = m_ref[idx][0:1, 0:1]
    c_prev = c_ref[idx]
    n_prev = n_ref[idx]
    a_row = ig_row - g_row
    h_out = None
    if q is not None:
        am = jnp.where(mask, a_row, NEG_BIG)
        c_col = jnp.maximum(m_prev, jnp.max(am, axis=-1, keepdims=True))
        c_b = jnp.broadcast_to(c_col, mask.shape)
        s = (jnp.dot(q, kt, preferred_element_type=F32) * jnp.exp2(am - c_b)).astype(BF16)
        w_state = jnp.exp2(m_prev - c_b)
        num = (jnp.dot(s, v, preferred_element_type=F32)
               + w_state * jnp.dot(q, c_prev.astype(BF16), preferred_element_type=F32))
        den = (jnp.dot(s, ones, preferred_element_type=F32)
               + w_state[:, :LANES] * jnp.dot(q, n_prev.astype(BF16), preferred_element_type=F32))
        den = jnp.maximum(jnp.abs(den), jnp.exp2(-(g_col + c_col)))
        h_out = num / jnp.concatenate([den] * (num.shape[-1] // LANES), axis=-1)
    w_row = b_tot + a_row
    m_new = jnp.maximum(b_tot + m_prev, jnp.max(w_row, axis=-1, keepdims=True))
    decay = jnp.exp2(b_tot + m_prev - m_new)
    kw = kt * jnp.exp2(w_row - m_new).astype(BF16)
    c_ref[idx] = decay * c_prev + jnp.dot(kw, v, preferred_element_type=F32)
    n_ref[idx] = decay * n_prev + jnp.dot(kw, ones, preferred_element_type=F32)
    m_ref[idx] = jnp.broadcast_to(m_new, m_ref.shape[1:])
    return h_out


def _mlstm_gates(gt_ref, gb_ref, tri):
    a = gt_ref[...] + gb_ref[...]
    lane = lax.broadcasted_iota(jnp.int32, a.shape, 1)
    is_forget = ((lane // M_HEADS) % 2) == 1
    act = jnp.where(is_forget, jax.nn.log_sigmoid(a), a) * LOG2_E
    cum = jnp.dot(tri, act, preferred_element_type=F32, precision=lax.Precision.HIGHEST)
    return cum, act.T, cum.T


def _mlstm_kernel(qf_ref, ktf_ref, vf_ref, gf_ref,
                  qb_ref, ktb_ref, vb_ref, gb_ref,
                  ktc_ref, vc_ref, gc_ref, gbias_ref,
                  hf_ref, hb_ref, c_ref, n_ref, m_ref):
    i = pl.program_id(1)
    L = MLSTM_CHUNK
    rows = lax.broadcasted_iota(jnp.int32, (L, L), 0)
    cols = lax.broadcasted_iota(jnp.int32, (L, L), 1)
    lower = cols <= rows
    upper = cols >= rows
    ones = jnp.ones((L, LANES), BF16)

    def run(q_ref, kt_ref, v_ref, g_ref, h_ref, direction):
        causal = direction == 0
        mask = lower if causal else upper
        cum, act_t, cum_t = _mlstm_gates(g_ref, gbias_ref, mask.astype(F32))
        for hd in range(M_HEADS):
            ci = 2 * M_HEADS * direction + hd
            cf = ci + M_HEADS
            g_col = cum[:, cf:cf + 1]
            b_tot = g_col[L - 1:L, :] if causal else g_col[0:1, :]
            ks = slice(hd * M_DQK, (hd + 1) * M_DQK)
            vs = slice(hd * M_DV, (hd + 1) * M_DV)
            q = None if q_ref is None else q_ref[:, ks]
            h = _mlstm_unit(q, kt_ref[ks, :], v_ref[:, vs], g_col, cum_t[cf:cf + 1, :], act_t[ci:ci + 1, :],
                            b_tot, mask, ones, c_ref, n_ref, m_ref, direction * M_HEADS + hd)
            if h is not None:
                h_ref[:, vs] = h.astype(h_ref.dtype)

    @pl.when(i == 0)
    def _():
        c_ref[...] = jnp.zeros_like(c_ref)
        n_ref[...] = jnp.zeros_like(n_ref)
        m_ref[...] = jnp.zeros_like(m_ref)
        run(None, ktc_ref, vc_ref, gc_ref, None, 0)
        run(None, ktc_ref, vc_ref, gc_ref, None, 1)

    @pl.when(i > 0)
    def _():
        run(qf_ref, ktf_ref, vf_ref, gf_ref, hf_ref, 0)
        run(qb_ref, ktb_ref, vb_ref, gb_ref, hb_ref, 1)


def _mlstm(q, kt, proj, small, ktc, proj_c, small_c, gate_bias):
    b, t, _ = q.shape
    L = MLSTM_CHUNK
    nc = t // L
    fwd = lambda bi, i: (bi, jnp.maximum(i - 1, 0), 0)
    bwd = lambda bi, i: (bi, jnp.minimum(nc - i, nc - 1), 0)
    fwd_t = lambda bi, i: (bi, 0, jnp.maximum(i - 1, 0))
    bwd_t = lambda bi, i: (bi, 0, jnp.minimum(nc - i, nc - 1))
    ctx = lambda bi, i: (bi, 0, 0)

    def specs(rm, tm_):
        return [pl.BlockSpec((None, L, M_QK_W), rm), pl.BlockSpec((None, M_QK_W, L), tm_),
                pl.BlockSpec((None, L, M_V_W), rm), pl.BlockSpec((None, L, LANES), rm)]

    in_specs = (specs(fwd, fwd_t) + specs(bwd, bwd_t)
                + [pl.BlockSpec((None, M_QK_W, L), ctx), pl.BlockSpec((None, L, M_V_W), ctx),
                   pl.BlockSpec((None, L, LANES), ctx), pl.BlockSpec((1, LANES), lambda bi, i: (0, 0))])
    return pl.pallas_call(
        _mlstm_kernel,
        grid=(b, nc + 1),
        in_specs=in_specs,
        out_specs=[pl.BlockSpec((None, L, M_V_W), fwd), pl.BlockSpec((None, L, M_V_W), bwd)],
        out_shape=[jax.ShapeDtypeStruct((b, t, M_V_W), BF16)] * 2,
        scratch_shapes=[pltpu.VMEM((2 * M_HEADS, M_DQK, M_DV), F32),
                        pltpu.VMEM((2 * M_HEADS, M_DQK, LANES), F32),
                        pltpu.VMEM((2 * M_HEADS, 8, LANES), F32)],
        compiler_params=_cparams(("parallel", "arbitrary")),
        name="mlstm",
    )(q, kt, proj, small, q, kt, proj, small, ktc, proj_c, small_c, gate_bias)


def _rope_partner_index():
    lane = np.arange(LANES)
    return np.where(lane % (2 * ROPE_FREQS) < ROPE_FREQS, lane + ROPE_FREQS, lane - ROPE_FREQS)


def _mla_prep_kernel(ckv_x_ref, sm_x_ref, ckv_c_ref, sm_c_ref, qa_ref,
                     wuk_ref, wuv_ref, kvg_ref, kgn_ref, kgr_ref, qgw_ref, qgs_ref, cos_ref, sin_ref,
                     pair_ref, sel_ref, half_ref, swap_ref, onecol_ref,
                     k_ref, v_ref, q_ref):
    i = pl.program_id(1)
    rows = ckv_x_ref.shape[0]
    lane_id = lax.broadcasted_iota(jnp.int32, (rows, LANES), 1)
    low_half = lane_id < A_ROPE
    low_half_of_pair = (lane_id % (2 * ROPE_FREQS)) < ROPE_FREQS
    inv_dim = 1.0 / A_QK

    def keys_values(ckv_ref, sm_ref, rotate):
        ckv = ckv_ref[...].astype(F32)
        cn = (ckv * lax.rsqrt(jnp.mean(ckv * ckv, axis=-1, keepdims=True) + EPS) * kvg_ref[...]).astype(BF16)
        v = jnp.dot(cn, wuv_ref[...], preferred_element_type=F32).astype(v_ref.dtype)
        ones_col = onecol_ref[...]
        for hd in range(A_HEADS):
            v_ref[:, hd * A_VX:hd * A_VX + A_DV] = v[:, hd * A_DV:(hd + 1) * A_DV]
            v_ref[:, hd * A_VX + A_DV:(hd + 1) * A_VX] = ones_col
        kn = jnp.dot(cn, wuk_ref[...], preferred_element_type=F32)
        sq = (kn * kn).astype(BF16)
        pair = pair_ref[...]
        ss = jnp.concatenate([jnp.dot(sq[:, g * 2 * A_NOPE:(g + 1) * 2 * A_NOPE], pair, preferred_element_type=F32)
                              for g in range(A_HEADS // 2)], axis=-1)
        kr2 = sm_ref[:, S_KR:S_KR + LANES]
        kr_ss = jnp.dot((kr2 * kr2).astype(BF16), half_ref[...], preferred_element_type=F32)
        r = lax.rsqrt((ss + jnp.concatenate([kr_ss] * (A_HEADS // 2), axis=-1)) * inv_dim + EPS)
        kns = kn * r * kgn_ref[...]
        krg = kr2 * kgr_ref[...]
        if rotate:
            partner = jnp.where(low_half_of_pair, pltpu.roll(krg, LANES - ROPE_FREQS, 1), pltpu.roll(krg, ROPE_FREQS, 1))
            krg = krg * cos_ref[...] + partner * sin_ref[...]
        for hd in range(A_HEADS):
            hs = slice(hd * A_NOPE, (hd + 1) * A_NOPE)
            kn_h = kns[:, hs]
            kr_h = krg * r[:, hs]
            if hd % 2 == 0:
                k_ref[hd, :, :LANES] = kn_h.astype(k_ref.dtype)
                k_ref[hd, :, LANES:] = jnp.where(low_half, kr_h, 0.0).astype(k_ref.dtype)
            else:
                k_ref[hd, :, :LANES] = jnp.where(low_half, 0.0, kn_h).astype(k_ref.dtype)
                k_ref[hd, :, LANES:] = jnp.where(low_half, kn_h, kr_h).astype(k_ref.dtype)

    @pl.when(i == 0)
    def _():
        keys_values(ckv_c_ref, sm_c_ref, False)

    @pl.when(i > 0)
    def _():
        keys_values(ckv_x_ref, sm_x_ref, True)
        scale = A_QK ** -0.5 * LOG2_E
        cos_e, sin_e = cos_ref[...], sin_ref[...]
        cos_o, sin_o = jnp.where(low_half, 1.0, cos_e), jnp.where(low_half, 0.0, sin_e)
        swap = swap_ref[...]
        for hd in range(A_HEADS):
            odd = hd % 2
            lo = (hd // 2) * 3 * LANES + odd * LANES
            win = qa_ref[:, lo:lo + A_QKX]
            wf = win.astype(F32)
            ss = jnp.dot((wf * wf).astype(BF16), sel_ref[odd], preferred_element_type=F32)
            r = lax.rsqrt(ss * inv_dim + EPS) * scale
            x1 = wf[:, :LANES] * qgw_ref[odd:odd + 1, :LANES]
            x2 = wf[:, LANES:] * qgw_ref[odd:odd + 1, LANES:]
            p2 = jnp.dot(win[:, LANES:], swap, preferred_element_type=F32) * qgs_ref[odd:odd + 1, :]
            rot2 = x2 * (cos_o if odd else cos_e) + p2 * (sin_o if odd else sin_e)
            q_ref[hd, :, :LANES] = (x1 * r).astype(q_ref.dtype)
            q_ref[hd, :, LANES:] = (rot2 * r).astype(q_ref.dtype)


def _mla_prep(proj, small, proj_c, small_c, w_uk, w_uv, kv_norm_g, k_norm_g, q_norm_g, cos_t, sin_t):
    b, t, _ = proj.shape
    tc = proj_c.shape[1]
    tr = tc
    nx = t // tr
    xrow = lambda bi, i: (bi, jnp.maximum(i - 1, 0), 0)
    const = lambda *shape: pl.BlockSpec(shape, lambda bi, i: (0,) * len(shape))

    halves = lambda a: a.reshape(a.shape[:-1] + (A_HEADS // 2, 2, 2, A_NOPE // 2))
    swap_odd = lambda a: jnp.concatenate([halves(a)[..., 0:1, :, :], halves(a)[..., 1:2, ::-1, :]],
                                         axis=-3).reshape(a.shape)
    w_uk_l = swap_odd(w_uk)
    gk_n, gk_r = k_norm_g[:A_NOPE], k_norm_g[A_NOPE:]
    gq_n, gq_r = q_norm_g[:A_NOPE], q_norm_g[A_NOPE:]
    kgn = swap_odd(jnp.tile(gk_n, A_HEADS)).reshape(1, A_HEADS * A_NOPE)
    kgr = jnp.tile(gk_r, 2).reshape(1, LANES)
    zeros = jnp.zeros((A_ROPE,), F32)
    qgw = jnp.stack([jnp.concatenate([gq_n, gq_r, zeros]), jnp.concatenate([zeros, gq_n, gq_r])])
    partner = _rope_partner_index()
    qgs = qgw[:, LANES:][:, partner]

    lane = np.arange(A_QKX)
    pair = (lane[:, None] // A_NOPE == lane[None, :] // A_NOPE).astype(np.float32)
    sel = np.stack([np.broadcast_to((lane < A_QK)[:, None], (A_QKX, LANES)),
                    np.broadcast_to((lane >= A_QKX - A_QK)[:, None], (A_QKX, LANES))]).astype(np.float32)
    half = np.broadcast_to((np.arange(LANES) < A_ROPE)[:, None], (LANES, A_QKX)).astype(np.float32)
    swap = np.zeros((LANES, LANES), np.float32)
    swap[partner, np.arange(LANES)] = 1.0
    onecol = np.zeros((tr, A_VX - A_DV), np.float32)
    onecol[:, 0] = 1.0
    as_bf16 = lambda a: jnp.asarray(a, BF16)

    return pl.pallas_call(
        _mla_prep_kernel,
        grid=(b, nx + 1),
        in_specs=[pl.BlockSpec((None, tr, KV_RANK), lambda bi, i: (bi, jnp.maximum(i - 1, 0), P_CKV // KV_RANK)),
                  pl.BlockSpec((None, tr, S_COLS), xrow),
                  pl.BlockSpec((None, tr, KV_RANK), lambda bi, i: (bi, 0, C_CKV // KV_RANK)),
                  pl.BlockSpec((None, tr, S_COLS), lambda bi, i: (bi, 0, 0)),
                  pl.BlockSpec((None, tr, A_Q_W), lambda bi, i: (bi, jnp.maximum(i - 1, 0), P_QA // A_Q_W)),
                  const(KV_RANK, A_HEADS * A_NOPE), const(KV_RANK, A_V_W), const(1, KV_RANK),
                  const(1, A_HEADS * A_NOPE), const(1, LANES), const(2, A_QKX), const(2, LANES),
                  pl.BlockSpec((tr, LANES), lambda bi, i: (jnp.maximum(i - 1, 0), 0)),
                  pl.BlockSpec((tr, LANES), lambda bi, i: (jnp.maximum(i - 1, 0), 0)),
                  const(A_QKX, A_QKX), const(2, A_QKX, LANES), const(LANES, A_QKX), const(LANES, LANES),
                  const(tr, A_VX - A_DV)],
        out_specs=[pl.BlockSpec((None, A_HEADS, tr, A_QKX), lambda bi, i: (bi, 0, i, 0)),
                   pl.BlockSpec((None, tr, A_HEADS * A_VX), lambda bi, i: (bi, i, 0)),
                   pl.BlockSpec((None, A_HEADS, tr, A_QKX), lambda bi, i: (bi, 0, jnp.maximum(i - 1, 0), 0))],
        out_shape=[jax.ShapeDtypeStruct((b, A_HEADS, tc + t, A_QKX), BF16),
                   jax.ShapeDtypeStruct((b, tc + t, A_HEADS * A_VX), BF16),
                   jax.ShapeDtypeStruct((b, A_HEADS, t, A_QKX), BF16)],
        compiler_params=_cparams(("parallel", "arbitrary")),
        name="mla_prep",
    )(proj, small, proj_c, small_c, proj, w_uk_l.astype(BF16), w_uv.astype(BF16), kv_norm_g.reshape(1, KV_RANK),
      kgn, kgr, qgw, qgs, cos_t, sin_t, as_bf16(pair), as_bf16(sel), as_bf16(half), as_bf16(swap), as_bf16(onecol))


def _attn_kernel(q_ref, k_ref, v_ref, o_ref, *, key_blocks):
    q = q_ref[...]
    m = acc = None
    for lo, hi in key_blocks:
        s = lax.dot_general(q, k_ref[lo:hi, :], (((1,), (1,)), ((), ())), preferred_element_type=F32)
        m_blk = jnp.max(s, axis=-1, keepdims=True)
        m_new = m_blk if m is None else jnp.maximum(m, m_blk)
        p = jnp.exp2(s - m_new).astype(BF16)
        pv = jnp.dot(p, v_ref[lo:hi, :], preferred_element_type=F32)
        acc = pv if acc is None else acc * jnp.exp2(m - m_new) + pv
        m = m_new
    o_ref[...] = (acc[:, :A_DV] / acc[:, A_DV:A_DV + 1]).astype(o_ref.dtype)


ATTN_KEY_BLOCK = 512


def _attention(q, k, v, n_ctx):
    b, h, s, _ = q.shape
    tk = k.shape[2]
    tq = min(1024, s)
    key_blocks =[(0, n_ctx)] + [(lo, lo + ATTN_KEY_BLOCK) for lo in range(n_ctx, tk, ATTN_KEY_BLOCK)]
    return pl.pallas_call(
        functools.partial(_attn_kernel, key_blocks=tuple(key_blocks)),
        grid=(b, h, s // tq),
        in_specs=[pl.BlockSpec((None, None, tq, A_QKX), lambda bi, hi, i: (bi, hi, i, 0)),
                  pl.BlockSpec((None, None, tk, A_QKX), lambda bi, hi, i: (bi, hi, 0, 0)),
                  pl.BlockSpec((None, tk, A_VX), lambda bi, hi, i: (bi, 0, hi))],
        out_specs=pl.BlockSpec((None, tq, A_DV), lambda bi, hi, i: (bi, i, hi)),
        out_shape=jax.ShapeDtypeStruct((b, s, A_V_W), BF16),
        compiler_params=_cparams(("parallel", "parallel", "parallel")),
        name="attention",
    )(q, k, v)


def _merge_kernel(hf_ref, hb_ref, om_ref, zm_ref, oa_ref, za_ref, gm_ref, ga_ref, mhg_ref,
                  wm_ref, wa_ref, o_ref, hm_ref):
    for hd in range(M_HEADS):
        vs = slice(hd * M_DV, (hd + 1) * M_DV)
        h = hf_ref[:, vs].astype(F32) + hb_ref[:, vs].astype(F32)
        hn = h * lax.rsqrt(jnp.mean(h * h, axis=-1, keepdims=True) + EPS) * mhg_ref[:, vs]
        gated = hn * jax.nn.sigmoid(om_ref[:, vs].astype(F32)) * _silu(zm_ref[:, vs].astype(F32))
        hm_ref[:, vs] = gated.astype(BF16)
    p_m = jnp.dot(hm_ref[...], wm_ref[...], preferred_element_type=F32)
    oa = (oa_ref[...].astype(F32) * _silu(za_ref[...].astype(F32))).astype(BF16)
    p_a = jnp.dot(oa, wa_ref[...], preferred_element_type=F32)
    y = jax.nn.sigmoid(gm_ref[...].astype(F32)) * p_m + jax.nn.sigmoid(ga_ref[...].astype(F32)) * p_a
    o_ref[...] = y.astype(o_ref.dtype)


def _merge(hf, hb, proj, oa, mh_norm_g, w_proj_m, w_proj_a):
    b, t, _ = hf.shape
    tm = 256
    row = lambda bi, i: (bi, i, 0)
    col = lambda c: (lambda bi, i: (bi, i, c))
    wspec = pl.BlockSpec((D_MODEL, D_MODEL), lambda bi, i: (0, 0))
    act = lambda im: pl.BlockSpec((None, tm, D_MODEL), im)
    return pl.pallas_call(
        _merge_kernel,
        grid=(b, t // tm),
        in_specs=[act(row), act(row),
                  act(col(P_OM // D_MODEL)), act(col(P_ZM // D_MODEL)),
                  act(row), act(col(P_ZA // D_MODEL)),
                  act(col(P_GM // D_MODEL)), act(col(P_GM // D_MODEL + 1)),
                  pl.BlockSpec((1, M_V_W), lambda bi, i: (0, 0)),
                  wspec, wspec],
        out_specs=act(row),
        out_shape=jax.ShapeDtypeStruct((b, t, D_MODEL), BF16),
        scratch_shapes=[pltpu.VMEM((tm, M_V_W), BF16)],
        compiler_params=_cparams(("parallel", "parallel")),
        name="merge",
    )(hf, hb, proj, proj, oa, proj, proj, proj, mh_norm_g.reshape(1, M_V_W), w_proj_m, w_proj_a)


def _out_kernel(y_ref, x_ref, gate_ref, w_ref, o_ref):
    o_ref[...] = x_ref[...] + gate_ref[...] * jnp.dot(y_ref[...], w_ref[...], preferred_element_type=F32)


def _out(y, x, gate, w_out):
    b, t, _ = x.shape
    tm = min(512, t)
    row = lambda bi, i: (bi, i, 0)
    return pl.pallas_call(
        _out_kernel,
        grid=(b, t // tm),
        in_specs=[pl.BlockSpec((None, tm, D_MODEL), row),
                  pl.BlockSpec((None, tm, D_MODEL), row),
                  pl.BlockSpec((None, 1, D_MODEL), lambda bi, i: (bi, 0, 0)),
                  pl.BlockSpec((D_MODEL, D_MODEL), lambda bi, i: (0, 0))],
        out_specs=pl.BlockSpec((None, tm, D_MODEL), row),
        out_shape=jax.ShapeDtypeStruct((b, t, D_MODEL), F32),
        compiler_params=_cparams(("parallel", "parallel")),
        name="out",
    )(y, x, gate, w_out)


def _cast_kernel(w_ref, o_ref):
    o_ref[...] = w_ref[...].astype(o_ref.dtype)


def _cast_bf16(w):
    rows, cols = w.shape
    tr = 128
    return pl.pallas_call(
        _cast_kernel,
        grid=(rows // tr,),
        in_specs=[pl.BlockSpec((tr, cols), lambda i: (i, 0))],
        out_specs=pl.BlockSpec((tr, cols), lambda i: (i, 0)),
        out_shape=jax.ShapeDtypeStruct((rows, cols), BF16),
        compiler_params=_cparams(("parallel",)),
        name="cast",
    )(w)


def _layout_w_in(w):
    w = _cast_bf16(w)
    km, vm, ckv, kr = w[:, _O_KM:_O_VM], w[:, _O_VM:_O_GT], w[:, _O_CKV:_O_KR], w[:, _O_KR:_O_QM]
    main = jnp.concatenate([vm, w[:, _O_OM:_O_ZM], w[:, _O_ZM:_O_QA], w[:, _O_ZA:_O_GM], w[:, _O_GM:_O_END],
                            w[:, _O_QA:_O_ZA], km, w[:, _O_QM:_O_OM], ckv], axis=1).astype(BF16)
    ctx = jnp.concatenate([vm, km, ckv], axis=1).astype(BF16)
    small = jnp.concatenate([w[:, _O_GT:_O_CKV], jnp.zeros((D_MODEL, S_KR - M_GATE_W), w.dtype), kr, kr],
                            axis=1).astype(BF16)
    return main, ctx, small


def _rope_tables(seq):
    pos = np.arange(seq)
    lane = np.arange(LANES) % A_ROPE
    axis = lane // (2 * ROPE_FREQS)
    half = (lane % (2 * ROPE_FREQS)) // ROPE_FREQS
    freqs = ROPE_THETA ** (-np.arange(ROPE_FREQS, dtype=np.float64) / ROPE_FREQS)
    coord = np.where((axis == 0)[None, :], (pos // GRID_W)[:, None], (pos % GRID_W)[:, None]).astype(np.float64)
    ang = coord * freqs[lane % ROPE_FREQS][None, :]
    sign = np.where(half == 0, -1.0, 1.0)[None, :]
    return jnp.asarray(np.cos(ang), F32), jnp.asarray(np.sin(ang) * sign, F32)


def _layer(x, c, ctx, c_ctx, ada_w, ada_b, norm_g, w_in, conv_w, conv_b, gate_b, mh_norm_g, q_norm_g, k_norm_g,
           kv_norm_g, w_uk, w_uv, w_proj_m, w_proj_a, w_out):
    b, t, _ = x.shape
    tc = ctx.shape[1]
    assert tc == MLSTM_CHUNK and t % MLSTM_CHUNK == 0 and t % GRID_W == 0

    c8 = jnp.zeros((8, D_MODEL), F32).at[:b].set(c).at[b].set(c_ctx)
    mod = _adaln(c8, ada_w, ada_b)
    shift, scale, gate = mod[:, :D_MODEL], mod[:, D_MODEL:2 * D_MODEL], mod[:, 2 * D_MODEL:]
    per_b = lambda a: a[:b].reshape(b, 1, D_MODEL)
    per_c = lambda a: jnp.broadcast_to(a[b].reshape(1, 1, D_MODEL), (b, 1, D_MODEL))

    w_main, w_ctx, w_small = _layout_w_in(w_in)
    proj, small = _proj(x, per_b(scale), per_b(shift), norm_g, w_main, w_small, P_COLS, 1280)
    proj_c, small_c = _proj(ctx, per_c(scale), per_c(shift), norm_g, w_ctx, w_small, C_COLS, C_COLS // 2)

    cw_q, cw_k = conv_w[:, :M_QK_W], conv_w[:, M_QK_W:]
    cb_q, cb_k = conv_b[:M_QK_W], conv_b[M_QK_W:]
    q_m = _conv(proj, P_QM, cw_q, cb_q, M_DQK ** -0.5, False)
    kt_m = _conv(proj, P_KM, cw_k, cb_k, 1.0, True)
    kt_c = _conv(proj_c, C_KM, cw_k, cb_k, 1.0, True)
    gate_bias = jnp.zeros((1, LANES), F32).at[0, :M_GATE_W].set(gate_b)
    hf, hb = _mlstm(q_m, kt_m, proj, small, kt_c, proj_c, small_c, gate_bias)

    cos_t, sin_t = _rope_tables(t)
    k_a, v_a, q_a = _mla_prep(proj, small, proj_c, small_c, w_uk, w_uv, kv_norm_g, k_norm_g, q_norm_g, cos_t, sin_t)
    o_a = _attention(q_a, k_a, v_a, tc)

    y = _merge(hf, hb, proj, o_a, mh_norm_g, w_proj_m.astype(BF16), w_proj_a.astype(BF16))
    return _out(y, x, per_b(gate), w_out.astype(BF16))


def kernel(x, c, ctx, c_ctx, ada_w, ada_b, norm_g, w_in, conv_w, conv_b, gate_b, mh_norm_g, q_norm_g, k_norm_g,
           kv_norm_g, w_uk, w_uv, w_proj_m, w_proj_a, w_out):
    assert ada_w.shape[0] == 1, "single-layer block"
    return _layer(x, c, ctx, c_ctx, ada_w[0], ada_b[0], norm_g[0], w_in[0], conv_w[0], conv_b[0], gate_b[0],
                  mh_norm_g[0], q_norm_g[0], k_norm_g[0], kv_norm_g[0], w_uk[0], w_uv[0], w_proj_m[0],
                  w_proj_a[0], w_out[0])
```

```python
import functools

import numpy as np
import jax
import jax.numpy as jnp
from jax import lax
from jax.experimental import pallas as pl
from jax.experimental.pallas import tpu as pltpu

F32 = jnp.float32
BF16 = jnp.bfloat16

D_MODEL = 2048
GRID_W = 64
EPS = 1e-6

M_HEADS = 8
M_DQK = 128
M_DV = 256
M_CONV = 5
A_HEADS = 16
A_NOPE = 128
A_ROPE = 64
A_QK = A_NOPE + A_ROPE
A_DV = 128
A_VX = 256
A_QKX = 256
KV_RANK = 512
ROPE_FREQS = A_ROPE // 4
ROPE_THETA = 10000.0

M_QK_W = M_HEADS * M_DQK
M_V_W = M_HEADS * M_DV
M_GATE_W = 4 * M_HEADS
A_Q_W = A_HEADS * A_QK
A_V_W = A_HEADS * A_DV

_O_KM = 0
_O_VM = _O_KM + M_QK_W
_O_GT = _O_VM + M_V_W
_O_CKV = _O_GT + M_GATE_W
_O_KR = _O_CKV + KV_RANK
_O_QM = _O_KR + A_ROPE
_O_OM = _O_QM + M_QK_W
_O_ZM = _O_OM + M_V_W
_O_QA = _O_ZM + M_V_W
_O_ZA = _O_QA + A_Q_W
_O_GM = _O_ZA + A_V_W
_O_END = _O_GM + 2 * D_MODEL

P_VM = 0
P_OM = P_VM + M_V_W
P_ZM = P_OM + M_V_W
P_ZA = P_ZM + M_V_W
P_GM = P_ZA + A_V_W
P_QA = P_GM + 2 * D_MODEL
P_KM = P_QA + A_Q_W
P_QM = P_KM + M_QK_W
P_CKV = P_QM + M_QK_W
P_COLS = P_CKV + KV_RANK
C_VM = 0
C_KM = C_VM + M_V_W
C_CKV = C_KM + M_QK_W
C_COLS = C_CKV + KV_RANK
S_GT = 0
S_KR = 128
S_COLS = 256

LANES = 128
MLSTM_CHUNK = 256
NEG_BIG = -1e30
LOG2_E = 1.4426950408889634
VMEM_LIMIT = 60 * 1024 * 1024


def _cparams(sem):
    return pltpu.CompilerParams(dimension_semantics=sem, vmem_limit_bytes=VMEM_LIMIT)


def _silu(a):
    return a * jax.nn.sigmoid(a)


def _adaln_kernel(c_ref, w_ref, b_ref, o_ref):
    s = _silu(c_ref[...])
    o_ref[...] = jnp.dot(s.astype(BF16), w_ref[...].astype(BF16), preferred_element_type=F32) + b_ref[...]


def _adaln(c8, ada_w, ada_b):
    n = ada_w.shape[1]
    tn = 1024
    return pl.pallas_call(
        _adaln_kernel,
        grid=(n // tn,),
        in_specs=[pl.BlockSpec((8, D_MODEL), lambda j: (0, 0)),
                  pl.BlockSpec((D_MODEL, tn), lambda j: (0, j)),
                  pl.BlockSpec((1, tn), lambda j: (0, j))],
        out_specs=pl.BlockSpec((8, tn), lambda j: (0, j)),
        out_shape=jax.ShapeDtypeStruct((8, n), F32),
        compiler_params=_cparams(("parallel",)),
        name="adaln",
    )(c8, ada_w, ada_b.reshape(1, n))


def _proj_kernel(x_ref, sc_ref, sh_ref, g_ref, w_ref, ws_ref, o_ref, os_ref, h_ref):
    @pl.when(pl.program_id(2) == 0)
    def _():
        x = x_ref[...]
        ms = jnp.mean(x * x, axis=-1, keepdims=True)
        y = x * lax.rsqrt(ms + EPS) * g_ref[...]
        h = (y * (1.0 + sc_ref[...]) + sh_ref[...]).astype(BF16)
        h_ref[...] = h
        os_ref[...] = jnp.dot(h, ws_ref[...], preferred_element_type=F32)

    o_ref[...] = jnp.dot(h_ref[...], w_ref[...], preferred_element_type=F32).astype(o_ref.dtype)


def _proj(x, scale, shift, norm_g, w_main, w_small, n_cols, tn):
    b, t, _ = x.shape
    tm = min(1024, t)
    return pl.pallas_call(
        _proj_kernel,
        grid=(b, t // tm, n_cols // tn),
        in_specs=[pl.BlockSpec((None, tm, D_MODEL), lambda bi, i, j: (bi, i, 0)),
                  pl.BlockSpec((None, 1, D_MODEL), lambda bi, i, j: (bi, 0, 0)),
                  pl.BlockSpec((None, 1, D_MODEL), lambda bi, i, j: (bi, 0, 0)),
                  pl.BlockSpec((1, D_MODEL), lambda bi, i, j: (0, 0)),
                  pl.BlockSpec((D_MODEL, tn), lambda bi, i, j: (0, j)),
                  pl.BlockSpec((D_MODEL, S_COLS), lambda bi, i, j: (0, 0))],
        out_specs=[pl.BlockSpec((None, tm, tn), lambda bi, i, j: (bi, i, j)),
                   pl.BlockSpec((None, tm, S_COLS), lambda bi, i, j: (bi, i, 0))],
        out_shape=[jax.ShapeDtypeStruct((b, t, n_cols), BF16),
                   jax.ShapeDtypeStruct((b, t, S_COLS), F32)],
        scratch_shapes=[pltpu.VMEM((tm, D_MODEL), BF16)],
        compiler_params=_cparams(("parallel", "parallel", "arbitrary")),
        name="proj",
    )(x, scale, shift, norm_g.reshape(1, D_MODEL), w_main, w_small)


CONV_HALO = 16


def _conv_kernel(prev_ref, cur_ref, next_ref, w_ref, b_ref, o_ref, buf_ref, *, tt, nt, out_scale, transpose):
    i = pl.program_id(1)
    buf_ref[CONV_HALO:CONV_HALO + tt, :] = cur_ref[...].astype(F32)
    buf_ref[0:CONV_HALO, :] = prev_ref[...].astype(F32) * (i > 0).astype(F32)
    buf_ref[CONV_HALO + tt:2 * CONV_HALO + tt, :] = next_ref[...].astype(F32) * (i < nt - 1).astype(F32)
    acc = jnp.broadcast_to(b_ref[...], (tt, cur_ref.shape[-1]))
    for k in range(M_CONV):
        lo = CONV_HALO - M_CONV // 2 + k
        acc = acc + w_ref[k:k + 1, :] * buf_ref[lo:lo + tt, :]
    y = _silu(acc)
    if out_scale != 1.0:
        y = y * out_scale
    o_ref[...] = (y.T if transpose else y).astype(o_ref.dtype)


def _conv(src, col_off, w, bias, out_scale, transpose):
    b, t, _ = src.shape
    cw = 512
    tt = min(512, t)
    nt = t // tt
    cb = col_off // cw
    hb = tt // CONV_HALO
    nhalo = t // CONV_HALO
    if transpose:
        out_shape = jax.ShapeDtypeStruct((b, M_QK_W, t), BF16)
        out_specs = pl.BlockSpec((None, cw, tt), lambda bi, i, c: (bi, c, i))
    else:
        out_shape = jax.ShapeDtypeStruct((b, t, M_QK_W), BF16)
        out_specs = pl.BlockSpec((None, tt, cw), lambda bi, i, c: (bi, i, c))
    return pl.pallas_call(
        functools.partial(_conv_kernel, tt=tt, nt=nt, out_scale=out_scale, transpose=transpose),
        grid=(b, nt, M_QK_W // cw),
        in_specs=[pl.BlockSpec((None, CONV_HALO, cw), lambda bi, i, c: (bi, jnp.maximum(i * hb - 1, 0), cb + c)),
                  pl.BlockSpec((None, tt, cw), lambda bi, i, c: (bi, i, cb + c)),
                  pl.BlockSpec((None, CONV_HALO, cw), lambda bi, i, c: (bi, jnp.minimum((i + 1) * hb, nhalo - 1), cb + c)),
                  pl.BlockSpec((M_CONV, cw), lambda bi, i, c: (0, c)),
                  pl.BlockSpec((1, cw), lambda bi, i, c: (0, c))],
        out_specs=out_specs,
        out_shape=out_shape,
        scratch_shapes=[pltpu.VMEM((tt + 2 * CONV_HALO, cw), F32)],
        compiler_params=_cparams(("parallel", "parallel", "parallel")),
        name="conv_t" if transpose else "conv",
    )(src, src, src, w, bias.reshape(1, M_QK_W))


def _mlstm_unit(q, kt, v, g_col, g_row, ig_row, b_tot, mask, ones, c_ref, n_ref, m_ref, idx):
    m_prev = m_ref[idx][0:1, 0:1]
    c_prev = c_ref[idx]
    n_prev = n_ref[idx]
    a_row = ig_row - g_row
    h_out = None
    if q is not None:
        am = jnp.where(mask, a_row, NEG_BIG)
        c_col = jnp.maximum(m_prev, jnp.max(am, axis=-1, keepdims=True))
        c_b = jnp.broadcast_to(c_col, mask.shape)
        s = (jnp.dot(q, kt, preferred_element_type=F32) * jnp.exp2(am - c_b)).astype(BF16)
        w_state = jnp.exp2(m_prev - c_b)
        num = (jnp.dot(s, v, preferred_element_type=F32)
               + w_state * jnp.dot(q, c_prev.astype(BF16), preferred_element_type=F32))
        den = (jnp.dot(s, ones, preferred_element_type=F32)
               + w_state[:, :LANES] * jnp.dot(q, n_prev.astype(BF16), preferred_element_type=F32))
        den = jnp.maximum(jnp.abs(den), jnp.exp2(-(g_col + c_col)))
        h_out = num / jnp.concatenate([den] * (num.shape[-1] // LANES), axis=-1)
    w_row = b_tot + a_row
    m_new = jnp.maximum(b_tot + m_prev, jnp.max(w_row, axis=-1, keepdims=True))
    decay = jnp.exp2(b_tot + m_prev - m_new)
    kw = kt * jnp.exp2(w_row - m_new).astype(BF16)
    c_ref[idx] = decay * c_prev + jnp.dot(kw, v, preferred_element_type=F32)
    n_ref[idx] = decay * n_prev + jnp.dot(kw, ones, preferred_element_type=F32)
    m_ref[idx] = jnp.broadcast_to(m_new, m_ref.shape[1:])
    return h_out


def _mlstm_gates(gt_ref, gb_ref, tri):
    a = gt_ref[...] + gb_ref[...]
    lane = lax.broadcasted_iota(jnp.int32, a.shape, 1)
    is_forget = ((lane // M_HEADS) % 2) == 1
    act = jnp.where(is_forget, jax.nn.log_sigmoid(a), a) * LOG2_E
    cum = jnp.dot(tri, act, preferred_element_type=F32, precision=lax.Precision.HIGHEST)
    return cum, act.T, cum.T


def _mlstm_kernel(qf_ref, ktf_ref, vf_ref, gf_ref,
                  qb_ref, ktb_ref, vb_ref, gb_ref,
                  ktc_ref, vc_ref, gc_ref, gbias_ref,
                  hf_ref, hb_ref, c_ref, n_ref, m_ref):
    i = pl.program_id(1)
    L = MLSTM_CHUNK
    rows = lax.broadcasted_iota(jnp.int32, (L, L), 0)
    cols = lax.broadcasted_iota(jnp.int32, (L, L), 1)
    lower = cols <= rows
    upper = cols >= rows
    ones = jnp.ones((L, LANES), BF16)

    def run(q_ref, kt_ref, v_ref, g_ref, h_ref, direction):
        causal = direction == 0
        mask = lower if causal else upper
        cum, act_t, cum_t = _mlstm_gates(g_ref, gbias_ref, mask.astype(F32))
        for hd in range(M_HEADS):
            ci = 2 * M_HEADS * direction + hd
            cf = ci + M_HEADS
            g_col = cum[:, cf:cf + 1]
            b_tot = g_col[L - 1:L, :] if causal else g_col[0:1, :]
            ks = slice(hd * M_DQK, (hd + 1) * M_DQK)
            vs = slice(hd * M_DV, (hd + 1) * M_DV)
            q = None if q_ref is None else q_ref[:, ks]
            h = _mlstm_unit(q, kt_ref[ks, :], v_ref[:, vs], g_col, cum_t[cf:cf + 1, :], act_t[ci:ci + 1, :],
                            b_tot, mask, ones, c_ref, n_ref, m_ref, direction * M_HEADS + hd)
            if h is not None:
                h_ref[:, vs] = h.astype(h_ref.dtype)

    @pl.when(i == 0)
    def _():
        c_ref[...] = jnp.zeros_like(c_ref)
        n_ref[...] = jnp.zeros_like(n_ref)
        m_ref[...] = jnp.zeros_like(m_ref)
        run(None, ktc_ref, vc_ref, gc_ref, None, 0)
        run(None, ktc_ref, vc_ref, gc_ref, None, 1)

    @pl.when(i > 0)
    def _():
        run(qf_ref, ktf_ref, vf_ref, gf_ref, hf_ref, 0)
        run(qb_ref, ktb_ref, vb_ref, gb_ref, hb_ref, 1)


def _mlstm(q, kt, proj, small, ktc, proj_c, small_c, gate_bias):
    b, t, _ = q.shape
    L = MLSTM_CHUNK
    nc = t // L
    fwd = lambda bi, i: (bi, jnp.maximum(i - 1, 0), 0)
    bwd = lambda bi, i: (bi, jnp.minimum(nc - i, nc - 1), 0)
    fwd_t = lambda bi, i: (bi, 0, jnp.maximum(i - 1, 0))
    bwd_t = lambda bi, i: (bi, 0, jnp.minimum(nc - i, nc - 1))
    ctx = lambda bi, i: (bi, 0, 0)

    def specs(rm, tm_):
        return [pl.BlockSpec((None, L, M_QK_W), rm), pl.BlockSpec((None, M_QK_W, L), tm_),
                pl.BlockSpec((None, L, M_V_W), rm), pl.BlockSpec((None, L, LANES), rm)]

    in_specs = (specs(fwd, fwd_t) + specs(bwd, bwd_t)
                + [pl.BlockSpec((None, M_QK_W, L), ctx), pl.BlockSpec((None, L, M_V_W), ctx),
                   pl.BlockSpec((None, L, LANES), ctx), pl.BlockSpec((1, LANES), lambda bi, i: (0, 0))])
    return pl.pallas_call(
        _mlstm_kernel,
        grid=(b, nc + 1),
        in_specs=in_specs,
        out_specs=[pl.BlockSpec((None, L, M_V_W), fwd), pl.BlockSpec((None, L, M_V_W), bwd)],
        out_shape=[jax.ShapeDtypeStruct((b, t, M_V_W), BF16)] * 2,
        scratch_shapes=[pltpu.VMEM((2 * M_HEADS, M_DQK, M_DV), F32),
                        pltpu.VMEM((2 * M_HEADS, M_DQK, LANES), F32),
                        pltpu.VMEM((2 * M_HEADS, 8, LANES), F32)],
        compiler_params=_cparams(("parallel", "arbitrary")),
        name="mlstm",
    )(q, kt, proj, small, q, kt, proj, small, ktc, proj_c, small_c, gate_bias)


def _rope_partner_index():
    lane = np.arange(LANES)
    return np.where(lane % (2 * ROPE_FREQS) < ROPE_FREQS, lane + ROPE_FREQS, lane - ROPE_FREQS)


def _mla_prep_kernel(ckv_x_ref, sm_x_ref, ckv_c_ref, sm_c_ref, qa_ref,
                     wuk_ref, wuv_ref, kvg_ref, kgn_ref, kgr_ref, qgw_ref, qgs_ref, cos_ref, sin_ref,
                     pair_ref, sel_ref, half_ref, swap_ref, onecol_ref,
                     k_ref, v_ref, q_ref):
    i = pl.program_id(1)
    rows = ckv_x_ref.shape[0]
    lane_id = lax.broadcasted_iota(jnp.int32, (rows, LANES), 1)
    low_half = lane_id < A_ROPE
    low_half_of_pair = (lane_id % (2 * ROPE_FREQS)) < ROPE_FREQS
    inv_dim = 1.0 / A_QK

    def keys_values(ckv_ref, sm_ref, rotate):
        ckv = ckv_ref[...].astype(F32)
        cn = (ckv * lax.rsqrt(jnp.mean(ckv * ckv, axis=-1, keepdims=True) + EPS) * kvg_ref[...]).astype(BF16)
        v = jnp.dot(cn, wuv_ref[...], preferred_element_type=F32).astype(v_ref.dtype)
        ones_col = onecol_ref[...]
        for hd in range(A_HEADS):
            v_ref[:, hd * A_VX:hd * A_VX + A_DV] = v[:, hd * A_DV:(hd + 1) * A_DV]
            v_ref[:, hd * A_VX + A_DV:(hd + 1) * A_VX] = ones_col
        kn = jnp.dot(cn, wuk_ref[...], preferred_element_type=F32)
        sq = (kn * kn).astype(BF16)
        pair = pair_ref[...]
        ss = jnp.concatenate([jnp.dot(sq[:, g * 2 * A_NOPE:(g + 1) * 2 * A_NOPE], pair, preferred_element_type=F32)
                              for g in range(A_HEADS // 2)], axis=-1)
        kr2 = sm_ref[:, S_KR:S_KR + LANES]
        kr_ss = jnp.dot((kr2 * kr2).astype(BF16), half_ref[...], preferred_element_type=F32)
        r = lax.rsqrt((ss + jnp.concatenate([kr_ss] * (A_HEADS // 2), axis=-1)) * inv_dim + EPS)
        kns = kn * r * kgn_ref[...]
        krg = kr2 * kgr_ref[...]
        if rotate:
            partner = jnp.where(low_half_of_pair, pltpu.roll(krg, LANES - ROPE_FREQS, 1), pltpu.roll(krg, ROPE_FREQS, 1))
            krg = krg * cos_ref[...] + partner * sin_ref[...]
        for hd in range(A_HEADS):
            hs = slice(hd * A_NOPE, (hd + 1) * A_NOPE)
            kn_h = kns[:, hs]
            kr_h = krg * r[:, hs]
            if hd % 2 == 0:
                k_ref[hd, :, :LANES] = kn_h.astype(k_ref.dtype)
                k_ref[hd, :, LANES:] = jnp.where(low_half, kr_h, 0.0).astype(k_ref.dtype)
            else:
                k_ref[hd, :, :LANES] = jnp.where(low_half, 0.0, kn_h).astype(k_ref.dtype)
                k_ref[hd, :, LANES:] = jnp.where(low_half, kn_h, kr_h).astype(k_ref.dtype)

    @pl.when(i == 0)
    def _():
        keys_values(ckv_c_ref, sm_c_ref, False)

    @pl.when(i > 0)
    def _():
        keys_values(ckv_x_ref, sm_x_ref, True)
        scale = A_QK ** -0.5 * LOG2_E
        cos_e, sin_e = cos_ref[...], sin_ref[...]
        cos_o, sin_o = jnp.where(low_half, 1.0, cos_e), jnp.where(low_half, 0.0, sin_e)
        swap = swap_ref[...]
        for hd in range(A_HEADS):
            odd = hd % 2
            lo = (hd // 2) * 3 * LANES + odd * LANES
            win = qa_ref[:, lo:lo + A_QKX]
            wf = win.astype(F32)
            ss = jnp.dot((wf * wf).astype(BF16), sel_ref[odd], preferred_element_type=F32)
            r = lax.rsqrt(ss * inv_dim + EPS) * scale
            x1 = wf[:, :LANES] * qgw_ref[odd:odd + 1, :LANES]
            x2 = wf[:, LANES:] * qgw_ref[odd:odd + 1, LANES:]
            p2 = jnp.dot(win[:, LANES:], swap, preferred_element_type=F32) * qgs_ref[odd:odd + 1, :]
            rot2 = x2 * (cos_o if odd else cos_e) + p2 * (sin_o if odd else sin_e)
            q_ref[hd, :, :LANES] = (x1 * r).astype(q_ref.dtype)
            q_ref[hd, :, LANES:] = (rot2 * r).astype(q_ref.dtype)


def _mla_prep(proj, small, proj_c, small_c, w_uk, w_uv, kv_norm_g, k_norm_g, q_norm_g, cos_t, sin_t):
    b, t, _ = proj.shape
    tc = proj_c.shape[1]
    tr = tc
    nx = t // tr
    xrow = lambda bi, i: (bi, jnp.maximum(i - 1, 0), 0)
    const = lambda *shape: pl.BlockSpec(shape, lambda bi, i: (0,) * len(shape))

    halves = lambda a: a.reshape(a.shape[:-1] + (A_HEADS // 2, 2, 2, A_NOPE // 2))
    swap_odd = lambda a: jnp.concatenate([halves(a)[..., 0:1, :, :], halves(a)[..., 1:2, ::-1, :]],
                                         axis=-3).reshape(a.shape)
    w_uk_l = swap_odd(w_uk)
    gk_n, gk_r = k_norm_g[:A_NOPE], k_norm_g[A_NOPE:]
    gq_n, gq_r = q_norm_g[:A_NOPE], q_norm_g[A_NOPE:]
    kgn = swap_odd(jnp.tile(gk_n, A_HEADS)).reshape(1, A_HEADS * A_NOPE)
    kgr = jnp.tile(gk_r, 2).reshape(1, LANES)
    zeros = jnp.zeros((A_ROPE,), F32)
    qgw = jnp.stack([jnp.concatenate([gq_n, gq_r, zeros]), jnp.concatenate([zeros, gq_n, gq_r])])
    partner = _rope_partner_index()
    qgs = qgw[:, LANES:][:, partner]

    lane = np.arange(A_QKX)
    pair = (lane[:, None] // A_NOPE == lane[None, :] // A_NOPE).astype(np.float32)
    sel = np.stack([np.broadcast_to((lane < A_QK)[:, None], (A_QKX, LANES)),
                    np.broadcast_to((lane >= A_QKX - A_QK)[:, None], (A_QKX, LANES))]).astype(np.float32)
    half = np.broadcast_to((np.arange(LANES) < A_ROPE)[:, None], (LANES, A_QKX)).astype(np.float32)
    swap = np.zeros((LANES, LANES), np.float32)
    swap[partner, np.arange(LANES)] = 1.0
    onecol = np.zeros((tr, A_VX - A_DV), np.float32)
    onecol[:, 0] = 1.0
    as_bf16 = lambda a: jnp.asarray(a, BF16)

    return pl.pallas_call(
        _mla_prep_kernel,
        grid=(b, nx + 1),
        in_specs=[pl.BlockSpec((None, tr, KV_RANK), lambda bi, i: (bi, jnp.maximum(i - 1, 0), P_CKV // KV_RANK)),
                  pl.BlockSpec((None, tr, S_COLS), xrow),
                  pl.BlockSpec((None, tr, KV_RANK), lambda bi, i: (bi, 0, C_CKV // KV_RANK)),
                  pl.BlockSpec((None, tr, S_COLS), lambda bi, i: (bi, 0, 0)),
                  pl.BlockSpec((None, tr, A_Q_W), lambda bi, i: (bi, jnp.maximum(i - 1, 0), P_QA // A_Q_W)),
                  const(KV_RANK, A_HEADS * A_NOPE), const(KV_RANK, A_V_W), const(1, KV_RANK),
                  const(1, A_HEADS * A_NOPE), const(1, LANES), const(2, A_QKX), const(2, LANES),
                  pl.BlockSpec((tr, LANES), lambda bi, i: (jnp.maximum(i - 1, 0), 0)),
                  pl.BlockSpec((tr, LANES), lambda bi, i: (jnp.maximum(i - 1, 0), 0)),
                  const(A_QKX, A_QKX), const(2, A_QKX, LANES), const(LANES, A_QKX), const(LANES, LANES),
                  const(tr, A_VX - A_DV)],
        out_specs=[pl.BlockSpec((None, A_HEADS, tr, A_QKX), lambda bi, i: (bi, 0, i, 0)),
                   pl.BlockSpec((None, tr, A_HEADS * A_VX), lambda bi, i: (bi, i, 0)),
                   pl.BlockSpec((None, A_HEADS, tr, A_QKX), lambda bi, i: (bi, 0, jnp.maximum(i - 1, 0), 0))],
        out_shape=[jax.ShapeDtypeStruct((b, A_HEADS, tc + t, A_QKX), BF16),
                   jax.ShapeDtypeStruct((b, tc + t, A_HEADS * A_VX), BF16),
                   jax.ShapeDtypeStruct((b, A_HEADS, t, A_QKX), BF16)],
        compiler_params=_cparams(("parallel", "arbitrary")),
        name="mla_prep",
    )(proj, small, proj_c, small_c, proj, w_uk_l.astype(BF16), w_uv.astype(BF16), kv_norm_g.reshape(1, KV_RANK),
      kgn, kgr, qgw, qgs, cos_t, sin_t, as_bf16(pair), as_bf16(sel), as_bf16(half), as_bf16(swap), as_bf16(onecol))


def _attn_kernel(q_ref, k_ref, v_ref, o_ref, *, key_blocks):
    q = q_ref[...]
    m = acc = None
    for lo, hi in key_blocks:
        s = lax.dot_general(q, k_ref[lo:hi, :], (((1,), (1,)), ((), ())), preferred_element_type=F32)
        m_blk = jnp.max(s, axis=-1, keepdims=True)
        m_new = m_blk if m is None else jnp.maximum(m, m_blk)
        p = jnp.exp2(s - m_new).astype(BF16)
        pv = jnp.dot(p, v_ref[lo:hi, :], preferred_element_type=F32)
        acc = pv if acc is None else acc * jnp.exp2(m - m_new) + pv
        m = m_new
    o_ref[...] = (acc[:, :A_DV] / acc[:, A_DV:A_DV + 1]).astype(o_ref.dtype)


ATTN_KEY_BLOCK = 512


def _attention(q, k, v, n_ctx):
    b, h, s, _ = q.shape
    tk = k.shape[2]
    tq = min(1024, s)
    key_blocks =[(0, n_ctx)] + [(lo, lo + ATTN_KEY_BLOCK) for lo in range(n_ctx, tk, ATTN_KEY_BLOCK)]
    return pl.pallas_call(
        functools.partial(_attn_kernel, key_blocks=tuple(key_blocks)),
        grid=(b, h, s // tq),
        in_specs=[pl.BlockSpec((None, None, tq, A_QKX), lambda bi, hi, i: (bi, hi, i, 0)),
                  pl.BlockSpec((None, None, tk, A_QKX), lambda bi, hi, i: (bi, hi, 0, 0)),
                  pl.BlockSpec((None, tk, A_VX), lambda bi, hi, i: (bi, 0, hi))],
        out_specs=pl.BlockSpec((None, tq, A_DV), lambda bi, hi, i: (bi, i, hi)),
        out_shape=jax.ShapeDtypeStruct((b, s, A_V_W), BF16),
        compiler_params=_cparams(("parallel", "parallel", "parallel")),
        name="attention",
    )(q, k, v)


def _merge_kernel(hf_ref, hb_ref, om_ref, zm_ref, oa_ref, za_ref, gm_ref, ga_ref, mhg_ref,
                  wm_ref, wa_ref, o_ref, hm_ref):
    for hd in range(M_HEADS):
        vs = slice(hd * M_DV, (hd + 1) * M_DV)
        h = hf_ref[:, vs].astype(F32) + hb_ref[:, vs].astype(F32)
        hn = h * lax.rsqrt(jnp.mean(h * h, axis=-1, keepdims=True) + EPS) * mhg_ref[:, vs]
        gated = hn * jax.nn.sigmoid(om_ref[:, vs].astype(F32)) * _silu(zm_ref[:, vs].astype(F32))
        hm_ref[:, vs] = gated.astype(BF16)
    p_m = jnp.dot(hm_ref[...], wm_ref[...], preferred_element_type=F32)
    oa = (oa_ref[...].astype(F32) * _silu(za_ref[...].astype(F32))).astype(BF16)
    p_a = jnp.dot(oa, wa_ref[...], preferred_element_type=F32)
    y = jax.nn.sigmoid(gm_ref[...].astype(F32)) * p_m + jax.nn.sigmoid(ga_ref[...].astype(F32)) * p_a
    o_ref[...] = y.astype(o_ref.dtype)


def _merge(hf, hb, proj, oa, mh_norm_g, w_proj_m, w_proj_a):
    b, t, _ = hf.shape
    tm = 256
    row = lambda bi, i: (bi, i, 0)
    col = lambda c: (lambda bi, i: (bi, i, c))
    wspec = pl.BlockSpec((D_MODEL, D_MODEL), lambda bi, i: (0, 0))
    act = lambda im: pl.BlockSpec((None, tm, D_MODEL), im)
    return pl.pallas_call(
        _merge_kernel,
        grid=(b, t // tm),
        in_specs=[act(row), act(row),
                  act(col(P_OM // D_MODEL)), act(col(P_ZM // D_MODEL)),
                  act(row), act(col(P_ZA // D_MODEL)),
                  act(col(P_GM // D_MODEL)), act(col(P_GM // D_MODEL + 1)),
                  pl.BlockSpec((1, M_V_W), lambda bi, i: (0, 0)),
                  wspec, wspec],
        out_specs=act(row),
        out_shape=jax.ShapeDtypeStruct((b, t, D_MODEL), BF16),
        scratch_shapes=[pltpu.VMEM((tm, M_V_W), BF16)],
        compiler_params=_cparams(("parallel", "parallel")),
        name="merge",
    )(hf, hb, proj, proj, oa, proj, proj, proj, mh_norm_g.reshape(1, M_V_W), w_proj_m, w_proj_a)


def _out_kernel(y_ref, x_ref, gate_ref, w_ref, o_ref):
    o_ref[...] = x_ref[...] + gate_ref[...] * jnp.dot(y_ref[...], w_ref[...], preferred_element_type=F32)


def _out(y, x, gate, w_out):
    b, t, _ = x.shape
    tm = min(512, t)
    row = lambda bi, i: (bi, i, 0)
    return pl.pallas_call(
        _out_kernel,
        grid=(b, t // tm),
        in_specs=[pl.BlockSpec((None, tm, D_MODEL), row),
                  pl.BlockSpec((None, tm, D_MODEL), row),
                  pl.BlockSpec((None, 1, D_MODEL), lambda bi, i: (bi, 0, 0)),
                  pl.BlockSpec((D_MODEL, D_MODEL), lambda bi, i: (0, 0))],
        out_specs=pl.BlockSpec((None, tm, D_MODEL), row),
        out_shape=jax.ShapeDtypeStruct((b, t, D_MODEL), F32),
        compiler_params=_cparams(("parallel", "parallel")),
        name="out",
    )(y, x, gate, w_out)


def _layout_w_in(w):
    km, vm, ckv, kr = w[:, _O_KM:_O_VM], w[:, _O_VM:_O_GT], w[:, _O_CKV:_O_KR], w[:, _O_KR:_O_QM]
    main = jnp.concatenate([vm, w[:, _O_OM:_O_ZM], w[:, _O_ZM:_O_QA], w[:, _O_ZA:_O_GM], w[:, _O_GM:_O_END],
                            w[:, _O_QA:_O_ZA], km, w[:, _O_QM:_O_OM], ckv], axis=1).astype(BF16)
    ctx = jnp.concatenate([vm, km, ckv], axis=1).astype(BF16)
    small = jnp.concatenate([w[:, _O_GT:_O_CKV], jnp.zeros((D_MODEL, S_KR - M_GATE_W), w.dtype), kr, kr],
                            axis=1).astype(BF16)
    return main, ctx, small


def _rope_tables(seq):
    pos = np.arange(seq)
    lane = np.arange(LANES) % A_ROPE
    axis = lane // (2 * ROPE_FREQS)
    half = (lane % (2 * ROPE_FREQS)) // ROPE_FREQS
    freqs = ROPE_THETA ** (-np.arange(ROPE_FREQS, dtype=np.float64) / ROPE_FREQS)
    coord = np.where((axis == 0)[None, :], (pos // GRID_W)[:, None], (pos % GRID_W)[:, None]).astype(np.float64)
    ang = coord * freqs[lane % ROPE_FREQS][None, :]
    sign = np.where(half == 0, -1.0, 1.0)[None, :]
    return jnp.asarray(np.cos(ang), F32), jnp.asarray(np.sin(ang) * sign, F32)


def _layer(x, c, ctx, c_ctx, ada_w, ada_b, norm_g, w_in, conv_w, conv_b, gate_b, mh_norm_g, q_norm_g, k_norm_g,
           kv_norm_g, w_uk, w_uv, w_proj_m, w_proj_a, w_out):
    b, t, _ = x.shape
    tc = ctx.shape[1]
    assert tc == MLSTM_CHUNK and t % MLSTM_CHUNK == 0 and t % GRID_W == 0

    c8 = jnp.zeros((8, D_MODEL), F32).at[:b].set(c).at[b].set(c_ctx)
    mod = _adaln(c8, ada_w, ada_b)
    shift, scale, gate = mod[:, :D_MODEL], mod[:, D_MODEL:2 * D_MODEL], mod[:, 2 * D_MODEL:]
    per_b = lambda a: a[:b].reshape(b, 1, D_MODEL)
    per_c = lambda a: jnp.broadcast_to(a[b].reshape(1, 1, D_MODEL), (b, 1, D_MODEL))

    w_main, w_ctx, w_small = _layout_w_in(w_in)
    proj, small = _proj(x, per_b(scale), per_b(shift), norm_g, w_main, w_small, P_COLS, 2560)
    proj_c, small_c = _proj(ctx, per_c(scale), per_c(shift), norm_g, w_ctx, w_small, C_COLS, C_COLS // 2)

    cw_q, cw_k = conv_w[:, :M_QK_W], conv_w[:, M_QK_W:]
    cb_q, cb_k = conv_b[:M_QK_W], conv_b[M_QK_W:]
    q_m = _conv(proj, P_QM, cw_q, cb_q, M_DQK ** -0.5, False)
    kt_m = _conv(proj, P_KM, cw_k, cb_k, 1.0, True)
    kt_c = _conv(proj_c, C_KM, cw_k, cb_k, 1.0, True)
    gate_bias = jnp.zeros((1, LANES), F32).at[0, :M_GATE_W].set(gate_b)
    hf, hb = _mlstm(q_m, kt_m, proj, small, kt_c, proj_c, small_c, gate_bias)

    cos_t, sin_t = _rope_tables(t)
    k_a, v_a, q_a = _mla_prep(proj, small, proj_c, small_c, w_uk, w_uv, kv_norm_g, k_norm_g, q_norm_g, cos_t, sin_t)
    o_a = _attention(q_a, k_a, v_a, tc)

    y = _merge(hf, hb, proj, o_a, mh_norm_g, w_proj_m.astype(BF16), w_proj_a.astype(BF16))
    return _out(y, x, per_b(gate), w_out.astype(BF16))


def kernel(x, c, ctx, c_ctx, ada_w, ada_b, norm_g, w_in, conv_w, conv_b, gate_b, mh_norm_g, q_norm_g, k_norm_g,
           kv_norm_g, w_uk, w_uv, w_proj_m, w_proj_a, w_out):
    assert ada_w.shape[0] == 1, "single-layer block"
    return _layer(x, c, ctx, c_ctx, ada_w[0], ada_b[0], norm_g[0], w_in[0], conv_w[0], conv_b[0], gate_b[0],
                  mh_norm_g[0], q_norm_g[0], k_norm_g[0], kv_norm_g[0], w_uk[0], w_uv[0], w_proj_m[0],
                  w_proj_a[0], w_out[0])
```

```python
import functools

import numpy as np
import jax
import jax.numpy as jnp
from jax import lax
from jax.experimental import pallas as pl
from jax.experimental.pallas import tpu as pltpu

F32 = jnp.float32
BF16 = jnp.bfloat16

D_MODEL = 2048
GRID_W = 64
EPS = 1e-6

M_HEADS = 8
M_DQK = 128
M_DV = 256
M_CONV = 5
A_HEADS = 16
A_NOPE = 128
A_ROPE = 64
A_QK = A_NOPE + A_ROPE
A_DV = 128
A_VX = 256
A_QKX = 256
KV_RANK = 512
ROPE_FREQS = A_ROPE // 4
ROPE_THETA = 10000.0

M_QK_W = M_HEADS * M_DQK
M_V_W = M_HEADS * M_DV
M_GATE_W = 4 * M_HEADS
A_Q_W = A_HEADS * A_QK
A_V_W = A_HEADS * A_DV

_O_KM = 0
_O_VM = _O_KM + M_QK_W
_O_GT = _O_VM + M_V_W
_O_CKV = _O_GT + M_GATE_W
_O_KR = _O_CKV + KV_RANK
_O_QM = _O_KR + A_ROPE
_O_OM = _O_QM + M_QK_W
_O_ZM = _O_OM + M_V_W
_O_QA = _O_ZM + M_V_W
_O_ZA = _O_QA + A_Q_W
_O_GM = _O_ZA + A_V_W
_O_END = _O_GM + 2 * D_MODEL

P_VM = 0
P_OM = P_VM + M_V_W
P_ZM = P_OM + M_V_W
P_ZA = P_ZM + M_V_W
P_GM = P_ZA + A_V_W
P_QA = P_GM + 2 * D_MODEL
P_KM = P_QA + A_Q_W
P_QM = P_KM + M_QK_W
P_CKV = P_QM + M_QK_W
P_COLS = P_CKV + KV_RANK
C_VM = 0
C_KM = C_VM + M_V_W
C_CKV = C_KM + M_QK_W
C_COLS = C_CKV + KV_RANK
S_GT = 0
S_KR = 128
S_COLS = 256

LANES = 128
MLSTM_CHUNK = 256
NEG_BIG = -1e30
LOG2_E = 1.4426950408889634
VMEM_LIMIT = 60 * 1024 * 1024


def _cparams(sem):
    return pltpu.CompilerParams(dimension_semantics=sem, vmem_limit_bytes=VMEM_LIMIT)


def _silu(a):
    return a * jax.nn.sigmoid(a)


def _adaln_kernel(c_ref, w_ref, b_ref, o_ref):
    s = _silu(c_ref[...])
    o_ref[...] = jnp.dot(s.astype(BF16), w_ref[...].astype(BF16), preferred_element_type=F32) + b_ref[...]


def _adaln(c8, ada_w, ada_b):
    n = ada_w.shape[1]
    tn = 1024
    return pl.pallas_call(
        _adaln_kernel,
        grid=(n // tn,),
        in_specs=[pl.BlockSpec((8, D_MODEL), lambda j: (0, 0)),
                  pl.BlockSpec((D_MODEL, tn), lambda j: (0, j)),
                  pl.BlockSpec((1, tn), lambda j: (0, j))],
        out_specs=pl.BlockSpec((8, tn), lambda j: (0, j)),
        out_shape=jax.ShapeDtypeStruct((8, n), F32),
        compiler_params=_cparams(("parallel",)),
        name="adaln",
    )(c8, ada_w, ada_b.reshape(1, n))


def _proj_kernel(x_ref, sc_ref, sh_ref, g_ref, w_ref, ws_ref, o_ref, os_ref, h_ref):
    @pl.when(pl.program_id(2) == 0)
    def _():
        x = x_ref[...]
        ms = jnp.mean(x * x, axis=-1, keepdims=True)
        y = x * lax.rsqrt(ms + EPS) * g_ref[...]
        h = (y * (1.0 + sc_ref[...]) + sh_ref[...]).astype(BF16)
        h_ref[...] = h
        os_ref[...] = jnp.dot(h, ws_ref[...], preferred_element_type=F32)

    o_ref[...] = jnp.dot(h_ref[...], w_ref[...], preferred_element_type=F32).astype(o_ref.dtype)


def _proj(x, scale, shift, norm_g, w_main, w_small, n_cols, tn):
    b, t, _ = x.shape
    tm = min(1024, t)
    return pl.pallas_call(
        _proj_kernel,
        grid=(b, t // tm, n_cols // tn),
        in_specs=[pl.BlockSpec((None, tm, D_MODEL), lambda bi, i, j: (bi, i, 0)),
                  pl.BlockSpec((None, 1, D_MODEL), lambda bi, i, j: (bi, 0, 0)),
                  pl.BlockSpec((None, 1, D_MODEL), lambda bi, i, j: (bi, 0, 0)),
                  pl.BlockSpec((1, D_MODEL), lambda bi, i, j: (0, 0)),
                  pl.BlockSpec((D_MODEL, tn), lambda bi, i, j: (0, j)),
                  pl.BlockSpec((D_MODEL, S_COLS), lambda bi, i, j: (0, 0))],
        out_specs=[pl.BlockSpec((None, tm, tn), lambda bi, i, j: (bi, i, j)),
                   pl.BlockSpec((None, tm, S_COLS), lambda bi, i, j: (bi, i, 0))],
        out_shape=[jax.ShapeDtypeStruct((b, t, n_cols), BF16),
                   jax.ShapeDtypeStruct((b, t, S_COLS), F32)],
        scratch_shapes=[pltpu.VMEM((tm, D_MODEL), BF16)],
        compiler_params=_cparams(("parallel", "parallel", "arbitrary")),
        name="proj",
    )(x, scale, shift, norm_g.reshape(1, D_MODEL), w_main, w_small)


CONV_HALO = 16


def _conv_kernel(prev_ref, cur_ref, next_ref, w_ref, b_ref, o_ref, buf_ref, *, tt, nt, out_scale, transpose):
    i = pl.program_id(1)
    buf_ref[CONV_HALO:CONV_HALO + tt, :] = cur_ref[...].astype(F32)
    buf_ref[0:CONV_HALO, :] = prev_ref[...].astype(F32) * (i > 0).astype(F32)
    buf_ref[CONV_HALO + tt:2 * CONV_HALO + tt, :] = next_ref[...].astype(F32) * (i < nt - 1).astype(F32)
    acc = jnp.broadcast_to(b_ref[...], (tt, cur_ref.shape[-1]))
    for k in range(M_CONV):
        lo = CONV_HALO - M_CONV // 2 + k
        acc = acc + w_ref[k:k + 1, :] * buf_ref[lo:lo + tt, :]
    y = _silu(acc)
    if out_scale != 1.0:
        y = y * out_scale
    o_ref[...] = (y.T if transpose else y).astype(o_ref.dtype)


def _conv(src, col_off, w, bias, out_scale, transpose):
    b, t, _ = src.shape
    cw = 512
    tt = min(512, t)
    nt = t // tt
    cb = col_off // cw
    hb = tt // CONV_HALO
    nhalo = t // CONV_HALO
    if transpose:
        out_shape = jax.ShapeDtypeStruct((b, M_QK_W, t), BF16)
        out_specs = pl.BlockSpec((None, cw, tt), lambda bi, i, c: (bi, c, i))
    else:
        out_shape = jax.ShapeDtypeStruct((b, t, M_QK_W), BF16)
        out_specs = pl.BlockSpec((None, tt, cw), lambda bi, i, c: (bi, i, c))
    return pl.pallas_call(
        functools.partial(_conv_kernel, tt=tt, nt=nt, out_scale=out_scale, transpose=transpose),
        grid=(b, nt, M_QK_W // cw),
        in_specs=[pl.BlockSpec((None, CONV_HALO, cw), lambda bi, i, c: (bi, jnp.maximum(i * hb - 1, 0), cb + c)),
                  pl.BlockSpec((None, tt, cw), lambda bi, i, c: (bi, i, cb + c)),
                  pl.BlockSpec((None, CONV_HALO, cw), lambda bi, i, c: (bi, jnp.minimum((i + 1) * hb, nhalo - 1), cb + c)),
                  pl.BlockSpec((M_CONV, cw), lambda bi, i, c: (0, c)),
                  pl.BlockSpec((1, cw), lambda bi, i, c: (0, c))],
        out_specs=out_specs,
        out_shape=out_shape,
        scratch_shapes=[pltpu.VMEM((tt + 2 * CONV_HALO, cw), F32)],
        compiler_params=_cparams(("parallel", "parallel", "parallel")),
        name="conv_t" if transpose else "conv",
    )(src, src, src, w, bias.reshape(1, M_QK_W))


def _mlstm_unit(q, kt, v, g_col, g_row, ig_row, b_tot, mask, ones, c_ref, n_ref, m_ref, idx):
    m_prev = m_ref[idx][0:1, 0:1]
    c_prev = c_ref[idx]
    n_prev = n_ref[idx]
    a_row = ig_row - g_row
    h_out = None
    if q is not None:
        am = jnp.where(mask, a_row, NEG_BIG)
        c_col = jnp.maximum(m_prev, jnp.max(am, axis=-1, keepdims=True))
        c_b = jnp.broadcast_to(c_col, mask.shape)
        s = (jnp.dot(q, kt, preferred_element_type=F32) * jnp.exp2(am - c_b)).astype(BF16)
        w_state = jnp.exp2(m_prev - c_b)
        num = (jnp.dot(s, v, preferred_element_type=F32)
               + w_state * jnp.dot(q, c_prev.astype(BF16), preferred_element_type=F32))
        den = (jnp.dot(s, ones, preferred_element_type=F32)
               + w_state[:, :LANES] * jnp.dot(q, n_prev.astype(BF16), preferred_element_type=F32))
        den = jnp.maximum(jnp.abs(den), jnp.exp2(-(g_col + c_col)))
        h_out = num / jnp.concatenate([den] * (num.shape[-1] // LANES), axis=-1)
    w_row = b_tot + a_row
    m_new = jnp.maximum(b_tot + m_prev, jnp.max(w_row, axis=-1, keepdims=True))
    decay = jnp.exp2(b_tot + m_prev - m_new)
    kw = kt * jnp.exp2(w_row - m_new).astype(BF16)
    c_ref[idx] = decay * c_prev + jnp.dot(kw, v, preferred_element_type=F32)
    n_ref[idx] = decay * n_prev + jnp.dot(kw, ones, preferred_element_type=F32)
    m_ref[idx] = jnp.broadcast_to(m_new, m_ref.shape[1:])
    return h_out


def _mlstm_gates(gt_ref, gb_ref, tri):
    a = gt_ref[...] + gb_ref[...]
    lane = lax.broadcasted_iota(jnp.int32, a.shape, 1)
    is_forget = ((lane // M_HEADS) % 2) == 1
    act = jnp.where(is_forget, jax.nn.log_sigmoid(a), a) * LOG2_E
    cum = jnp.dot(tri, act, preferred_element_type=F32, precision=lax.Precision.HIGHEST)
    return cum, act.T, cum.T


def _mlstm_kernel(qf_ref, ktf_ref, vf_ref, gf_ref,
                  qb_ref, ktb_ref, vb_ref, gb_ref,
                  ktc_ref, vc_ref, gc_ref, gbias_ref,
                  hf_ref, hb_ref, c_ref, n_ref, m_ref):
    i = pl.program_id(1)
    L = MLSTM_CHUNK
    rows = lax.broadcasted_iota(jnp.int32, (L, L), 0)
    cols = lax.broadcasted_iota(jnp.int32, (L, L), 1)
    lower = cols <= rows
    upper = cols >= rows
    ones = jnp.ones((L, LANES), BF16)

    def run(q_ref, kt_ref, v_ref, g_ref, h_ref, direction):
        causal = direction == 0
        mask = lower if causal else upper
        cum, act_t, cum_t = _mlstm_gates(g_ref, gbias_ref, mask.astype(F32))
        for hd in range(M_HEADS):
            ci = 2 * M_HEADS * direction + hd
            cf = ci + M_HEADS
            g_col = cum[:, cf:cf + 1]
            b_tot = g_col[L - 1:L, :] if causal else g_col[0:1, :]
            ks = slice(hd * M_DQK, (hd + 1) * M_DQK)
            vs = slice(hd * M_DV, (hd + 1) * M_DV)
            q = None if q_ref is None else q_ref[:, ks]
            h = _mlstm_unit(q, kt_ref[ks, :], v_ref[:, vs], g_col, cum_t[cf:cf + 1, :], act_t[ci:ci + 1, :],
                            b_tot, mask, ones, c_ref, n_ref, m_ref, direction * M_HEADS + hd)
            if h is not None:
                h_ref[:, vs] = h.astype(h_ref.dtype)

    @pl.when(i == 0)
    def _():
        c_ref[...] = jnp.zeros_like(c_ref)
        n_ref[...] = jnp.zeros_like(n_ref)
        m_ref[...] = jnp.zeros_like(m_ref)
        run(None, ktc_ref, vc_ref, gc_ref, None, 0)
        run(None, ktc_ref, vc_ref, gc_ref, None, 1)

    @pl.when(i > 0)
    def _():
        run(qf_ref, ktf_ref, vf_ref, gf_ref, hf_ref, 0)
        run(qb_ref, ktb_ref, vb_ref, gb_ref, hb_ref, 1)


def _mlstm(q, kt, proj, small, ktc, proj_c, small_c, gate_bias):
    b, t, _ = q.shape
    L = MLSTM_CHUNK
    nc = t // L
    fwd = lambda bi, i: (bi, jnp.maximum(i - 1, 0), 0)
    bwd = lambda bi, i: (bi, jnp.minimum(nc - i, nc - 1), 0)
    fwd_t = lambda bi, i: (bi, 0, jnp.maximum(i - 1, 0))
    bwd_t = lambda bi, i: (bi, 0, jnp.minimum(nc - i, nc - 1))
    ctx = lambda bi, i: (bi, 0, 0)

    def specs(rm, tm_):
        return [pl.BlockSpec((None, L, M_QK_W), rm), pl.BlockSpec((None, M_QK_W, L), tm_),
                pl.BlockSpec((None, L, M_V_W), rm), pl.BlockSpec((None, L, LANES), rm)]

    in_specs = (specs(fwd, fwd_t) + specs(bwd, bwd_t)
                + [pl.BlockSpec((None, M_QK_W, L), ctx), pl.BlockSpec((None, L, M_V_W), ctx),
                   pl.BlockSpec((None, L, LANES), ctx), pl.BlockSpec((1, LANES), lambda bi, i: (0, 0))])
    return pl.pallas_call(
        _mlstm_kernel,
        grid=(b, nc + 1),
        in_specs=in_specs,
        out_specs=[pl.BlockSpec((None, L, M_V_W), fwd), pl.BlockSpec((None, L, M_V_W), bwd)],
        out_shape=[jax.ShapeDtypeStruct((b, t, M_V_W), BF16)] * 2,
        scratch_shapes=[pltpu.VMEM((2 * M_HEADS, M_DQK, M_DV), F32),
                        pltpu.VMEM((2 * M_HEADS, M_DQK, LANES), F32),
                        pltpu.VMEM((2 * M_HEADS, 8, LANES), F32)],
        compiler_params=_cparams(("parallel", "arbitrary")),
        name="mlstm",
    )(q, kt, proj, small, q, kt, proj, small, ktc, proj_c, small_c, gate_bias)


def _rope_partner_index():
    lane = np.arange(LANES)
    return np.where(lane % (2 * ROPE_FREQS) < ROPE_FREQS, lane + ROPE_FREQS, lane - ROPE_FREQS)


def _mla_prep_kernel(ckv_x_ref, sm_x_ref, ckv_c_ref, sm_c_ref, qa_ref,
                     wuk_ref, wuv_ref, kvg_ref, kgn_ref, kgr_ref, qgw_ref, qgs_ref, cos_ref, sin_ref,
                     pair_ref, sel_ref, half_ref, swap_ref, onecol_ref,
                     k_ref, v_ref, q_ref):
    i = pl.program_id(1)
    rows = ckv_x_ref.shape[0]
    lane_id = lax.broadcasted_iota(jnp.int32, (rows, LANES), 1)
    low_half = lane_id < A_ROPE
    low_half_of_pair = (lane_id % (2 * ROPE_FREQS)) < ROPE_FREQS
    eps_dim = A_QK * EPS

    def keys_values(ckv_ref, sm_ref, rotate):
        ckv = ckv_ref[...].astype(F32)
        cn = (ckv * lax.rsqrt(jnp.mean(ckv * ckv, axis=-1, keepdims=True) + EPS) * kvg_ref[...]).astype(BF16)
        v = jnp.dot(cn, wuv_ref[...], preferred_element_type=F32).astype(v_ref.dtype)
        ones_col = onecol_ref[...]
        for hd in range(A_HEADS):
            v_ref[:, hd * A_VX:hd * A_VX + A_DV] = v[:, hd * A_DV:(hd + 1) * A_DV]
            v_ref[:, hd * A_VX + A_DV:(hd + 1) * A_VX] = ones_col
        kn = jnp.dot(cn, wuk_ref[...], preferred_element_type=F32)
        sq = (kn * kn).astype(BF16)
        pair = pair_ref[...]
        ss = jnp.concatenate([jnp.dot(sq[:, g * 2 * A_NOPE:(g + 1) * 2 * A_NOPE], pair, preferred_element_type=F32)
                              for g in range(A_HEADS // 2)], axis=-1)
        kr2 = sm_ref[:, S_KR:S_KR + LANES]
        kr_ss = jnp.dot((kr2 * kr2).astype(BF16), half_ref[...], preferred_element_type=F32)
        r = lax.rsqrt(ss + jnp.concatenate([kr_ss + eps_dim] * (A_HEADS // 2), axis=-1))
        kns = kn * r * kgn_ref[...]
        krg = kr2 * kgr_ref[...]
        if rotate:
            partner = jnp.where(low_half_of_pair, pltpu.roll(krg, LANES - ROPE_FREQS, 1), pltpu.roll(krg, ROPE_FREQS, 1))
            krg = krg * cos_ref[...] + partner * sin_ref[...]
        for hd in range(A_HEADS):
            hs = slice(hd * A_NOPE, (hd + 1) * A_NOPE)
            kn_h = kns[:, hs]
            kr_h = krg * r[:, hs]
            if hd % 2 == 0:
                k_ref[hd, :, :LANES] = kn_h.astype(k_ref.dtype)
                k_ref[hd, :, LANES:] = jnp.where(low_half, kr_h, 0.0).astype(k_ref.dtype)
            else:
                k_ref[hd, :, :LANES] = jnp.where(low_half, 0.0, kn_h).astype(k_ref.dtype)
                k_ref[hd, :, LANES:] = jnp.where(low_half, kn_h, kr_h).astype(k_ref.dtype)

    @pl.when(i == 0)
    def _():
        keys_values(ckv_c_ref, sm_c_ref, False)

    @pl.when(i > 0)
    def _():
        keys_values(ckv_x_ref, sm_x_ref, True)
        cos_e, sin_e = cos_ref[...], sin_ref[...]
        cos_o, sin_o = jnp.where(low_half, 1.0, cos_e), jnp.where(low_half, 0.0, sin_e)
        swap = swap_ref[...]
        for hd in range(A_HEADS):
            odd = hd % 2
            lo = (hd // 2) * 3 * LANES + odd * LANES
            win = qa_ref[:, lo:lo + A_QKX]
            wf = win.astype(F32)
            ss = jnp.dot((wf * wf).astype(BF16), sel_ref[odd], preferred_element_type=F32)
            r = lax.rsqrt(ss + eps_dim)
            x1 = wf[:, :LANES] * qgw_ref[odd:odd + 1, :LANES]
            x2 = wf[:, LANES:] * qgw_ref[odd:odd + 1, LANES:]
            p2 = jnp.dot(win[:, LANES:], swap, preferred_element_type=F32) * qgs_ref[odd:odd + 1, :]
            rot2 = x2 * (cos_o if odd else cos_e) + p2 * (sin_o if odd else sin_e)
            q_ref[hd, :, :LANES] = (x1 * r).astype(q_ref.dtype)
            q_ref[hd, :, LANES:] = (rot2 * r).astype(q_ref.dtype)


def _mla_prep(proj, small, proj_c, small_c, w_uk, w_uv, kv_norm_g, k_norm_g, q_norm_g, cos_t, sin_t):
    b, t, _ = proj.shape
    tc = proj_c.shape[1]
    tr = tc
    nx = t // tr
    xrow = lambda bi, i: (bi, jnp.maximum(i - 1, 0), 0)
    const = lambda *shape: pl.BlockSpec(shape, lambda bi, i: (0,) * len(shape))

    halves = lambda a: a.reshape(a.shape[:-1] + (A_HEADS // 2, 2, 2, A_NOPE // 2))
    swap_odd = lambda a: jnp.concatenate([halves(a)[..., 0:1, :, :], halves(a)[..., 1:2, ::-1, :]],
                                         axis=-3).reshape(a.shape)
    w_uk_l = swap_odd(w_uk)
    gk_n, gk_r = k_norm_g[:A_NOPE] * A_QK ** 0.5, k_norm_g[A_NOPE:] * A_QK ** 0.5
    gq_n, gq_r = q_norm_g[:A_NOPE] * LOG2_E, q_norm_g[A_NOPE:] * LOG2_E
    kgn = swap_odd(jnp.tile(gk_n, A_HEADS)).reshape(1, A_HEADS * A_NOPE)
    kgr = jnp.tile(gk_r, 2).reshape(1, LANES)
    zeros = jnp.zeros((A_ROPE,), F32)
    qgw = jnp.stack([jnp.concatenate([gq_n, gq_r, zeros]), jnp.concatenate([zeros, gq_n, gq_r])])
    partner = _rope_partner_index()
    qgs = qgw[:, LANES:][:, partner]

    lane = np.arange(A_QKX)
    pair = (lane[:, None] // A_NOPE == lane[None, :] // A_NOPE).astype(np.float32)
    sel = np.stack([np.broadcast_to((lane < A_QK)[:, None], (A_QKX, LANES)),
                    np.broadcast_to((lane >= A_QKX - A_QK)[:, None], (A_QKX, LANES))]).astype(np.float32)
    half = np.broadcast_to((np.arange(LANES) < A_ROPE)[:, None], (LANES, A_QKX)).astype(np.float32)
    swap = np.zeros((LANES, LANES), np.float32)
    swap[partner, np.arange(LANES)] = 1.0
    onecol = np.zeros((tr, A_VX - A_DV), np.float32)
    onecol[:, 0] = 1.0
    as_bf16 = lambda a: jnp.asarray(a, BF16)

    return pl.pallas_call(
        _mla_prep_kernel,
        grid=(b, nx + 1),
        in_specs=[pl.BlockSpec((None, tr, KV_RANK), lambda bi, i: (bi, jnp.maximum(i - 1, 0), P_CKV // KV_RANK)),
                  pl.BlockSpec((None, tr, S_COLS), xrow),
                  pl.BlockSpec((None, tr, KV_RANK), lambda bi, i: (bi, 0, C_CKV // KV_RANK)),
                  pl.BlockSpec((None, tr, S_COLS), lambda bi, i: (bi, 0, 0)),
                  pl.BlockSpec((None, tr, A_Q_W), lambda bi, i: (bi, jnp.maximum(i - 1, 0), P_QA // A_Q_W)),
                  const(KV_RANK, A_HEADS * A_NOPE), const(KV_RANK, A_V_W), const(1, KV_RANK),
                  const(1, A_HEADS * A_NOPE), const(1, LANES), const(2, A_QKX), const(2, LANES),
                  pl.BlockSpec((tr, LANES), lambda bi, i: (jnp.maximum(i - 1, 0), 0)),
                  pl.BlockSpec((tr, LANES), lambda bi, i: (jnp.maximum(i - 1, 0), 0)),
                  const(A_QKX, A_QKX), const(2, A_QKX, LANES), const(LANES, A_QKX), const(LANES, LANES),
                  const(tr, A_VX - A_DV)],
        out_specs=[pl.BlockSpec((None, A_HEADS, tr, A_QKX), lambda bi, i: (bi, 0, i, 0)),
                   pl.BlockSpec((None, tr, A_HEADS * A_VX), lambda bi, i: (bi, i, 0)),
                   pl.BlockSpec((None, A_HEADS, tr, A_QKX), lambda bi, i: (bi, 0, jnp.maximum(i - 1, 0), 0))],
        out_shape=[jax.ShapeDtypeStruct((b, A_HEADS, tc + t, A_QKX), BF16),
                   jax.ShapeDtypeStruct((b, tc + t, A_HEADS * A_VX), BF16),
                   jax.ShapeDtypeStruct((b, A_HEADS, t, A_QKX), BF16)],
        compiler_params=_cparams(("parallel", "arbitrary")),
        name="mla_prep",
    )(proj, small, proj_c, small_c, proj, w_uk_l.astype(BF16), w_uv.astype(BF16), kv_norm_g.reshape(1, KV_RANK),
      kgn, kgr, qgw, qgs, cos_t, sin_t, as_bf16(pair), as_bf16(sel), as_bf16(half), as_bf16(swap), as_bf16(onecol))


def _attn_kernel(q_ref, k_ref, v_ref, o_ref, *, key_blocks):
    q = q_ref[...]
    m = acc = None
    for lo, hi in key_blocks:
        s = lax.dot_general(q, k_ref[lo:hi, :], (((1,), (1,)), ((), ())), preferred_element_type=F32)
        m_blk = jnp.max(s, axis=-1, keepdims=True)
        m_new = m_blk if m is None else jnp.maximum(m, m_blk)
        p = jnp.exp2(s - m_new).astype(BF16)
        pv = jnp.dot(p, v_ref[lo:hi, :], preferred_element_type=F32)
        acc = pv if acc is None else acc * jnp.exp2(m - m_new) + pv
        m = m_new
    o_ref[...] = (acc[:, :A_DV] / acc[:, A_DV:A_DV + 1]).astype(o_ref.dtype)


ATTN_KEY_BLOCK = 256


def _attention(q, k, v, n_ctx):
    b, h, s, _ = q.shape
    tk = k.shape[2]
    tq = min(1024, s)
    key_blocks = [(lo, lo + ATTN_KEY_BLOCK) for lo in range(n_ctx, tk, ATTN_KEY_BLOCK)] + [(0, n_ctx)]
    return pl.pallas_call(
        functools.partial(_attn_kernel, key_blocks=tuple(key_blocks)),
        grid=(b, h, s // tq),
        in_specs=[pl.BlockSpec((None, None, tq, A_QKX), lambda bi, hi, i: (bi, hi, i, 0)),
                  pl.BlockSpec((None, None, tk, A_QKX), lambda bi, hi, i: (bi, hi, 0, 0)),
                  pl.BlockSpec((None, tk, A_VX), lambda bi, hi, i: (bi, 0, hi))],
        out_specs=pl.BlockSpec((None, tq, A_DV), lambda bi, hi, i: (bi, i, hi)),
        out_shape=jax.ShapeDtypeStruct((b, s, A_V_W), BF16),
        compiler_params=_cparams(("parallel", "parallel", "parallel")),
        name="attention",
    )(q, k, v)


def _merge_kernel(hf_ref, hb_ref, om_ref, zm_ref, oa_ref, za_ref, gm_ref, ga_ref, mhg_ref,
                  wm_ref, wa_ref, o_ref, hm_ref):
    for hd in range(M_HEADS):
        vs = slice(hd * M_DV, (hd + 1) * M_DV)
        h = hf_ref[:, vs].astype(F32) + hb_ref[:, vs].astype(F32)
        hn = h * lax.rsqrt(jnp.mean(h * h, axis=-1, keepdims=True) + EPS) * mhg_ref[:, vs]
        gated = hn * jax.nn.sigmoid(om_ref[:, vs].astype(F32)) * _silu(zm_ref[:, vs].astype(F32))
        hm_ref[:, vs] = gated.astype(BF16)
    p_m = jnp.dot(hm_ref[...], wm_ref[...], preferred_element_type=F32)
    oa = (oa_ref[...].astype(F32) * _silu(za_ref[...].astype(F32))).astype(BF16)
    p_a = jnp.dot(oa, wa_ref[...], preferred_element_type=F32)
    y = jax.nn.sigmoid(gm_ref[...].astype(F32)) * p_m + jax.nn.sigmoid(ga_ref[...].astype(F32)) * p_a
    o_ref[...] = y.astype(o_ref.dtype)


def _merge(hf, hb, proj, oa, mh_norm_g, w_proj_m, w_proj_a):
    b, t, _ = hf.shape
    tm = 256
    row = lambda bi, i: (bi, i, 0)
    col = lambda c: (lambda bi, i: (bi, i, c))
    wspec = pl.BlockSpec((D_MODEL, D_MODEL), lambda bi, i: (0, 0))
    act = lambda im: pl.BlockSpec((None, tm, D_MODEL), im)
    return pl.pallas_call(
        _merge_kernel,
        grid=(b, t // tm),
        in_specs=[act(row), act(row),
                  act(col(P_OM // D_MODEL)), act(col(P_ZM // D_MODEL)),
                  act(row), act(col(P_ZA // D_MODEL)),
                  act(col(P_GM // D_MODEL)), act(col(P_GM // D_MODEL + 1)),
                  pl.BlockSpec((1, M_V_W), lambda bi, i: (0, 0)),
                  wspec, wspec],
        out_specs=act(row),
        out_shape=jax.ShapeDtypeStruct((b, t, D_MODEL), BF16),
        scratch_shapes=[pltpu.VMEM((tm, M_V_W), BF16)],
        compiler_params=_cparams(("parallel", "parallel")),
        name="merge",
    )(hf, hb, proj, proj, oa, proj, proj, proj, mh_norm_g.reshape(1, M_V_W), w_proj_m, w_proj_a)


def _out_kernel(y_ref, x_ref, gate_ref, w_ref, o_ref):
    o_ref[...] = x_ref[...] + gate_ref[...] * jnp.dot(y_ref[...], w_ref[...], preferred_element_type=F32)


def _out(y, x, gate, w_out):
    b, t, _ = x.shape
    tm = min(512, t)
    row = lambda bi, i: (bi, i, 0)
    return pl.pallas_call(
        _out_kernel,
        grid=(b, t // tm),
        in_specs=[pl.BlockSpec((None, tm, D_MODEL), row),
                  pl.BlockSpec((None, tm, D_MODEL), row),
                  pl.BlockSpec((None, 1, D_MODEL), lambda bi, i: (bi, 0, 0)),
                  pl.BlockSpec((D_MODEL, D_MODEL), lambda bi, i: (0, 0))],
        out_specs=pl.BlockSpec((None, tm, D_MODEL), row),
        out_shape=jax.ShapeDtypeStruct((b, t, D_MODEL), F32),
        compiler_params=_cparams(("parallel", "parallel")),
        name="out",
    )(y, x, gate, w_out)


def _layout_w_in(w):
    km, vm, ckv, kr = w[:, _O_KM:_O_VM], w[:, _O_VM:_O_GT], w[:, _O_CKV:_O_KR], w[:, _O_KR:_O_QM]
    main = jnp.concatenate([vm, w[:, _O_OM:_O_ZM], w[:, _O_ZM:_O_QA], w[:, _O_ZA:_O_GM], w[:, _O_GM:_O_END],
                            w[:, _O_QA:_O_ZA], km, w[:, _O_QM:_O_OM], ckv], axis=1).astype(BF16)
    ctx = jnp.concatenate([vm, km, ckv], axis=1).astype(BF16)
    small = jnp.concatenate([w[:, _O_GT:_O_CKV], jnp.zeros((D_MODEL, S_KR - M_GATE_W), w.dtype), kr, kr],
                            axis=1).astype(BF16)
    return main, ctx, small


def _rope_tables(seq):
    pos = np.arange(seq)
    lane = np.arange(LANES) % A_ROPE
    axis = lane // (2 * ROPE_FREQS)
    half = (lane % (2 * ROPE_FREQS)) // ROPE_FREQS
    freqs = ROPE_THETA ** (-np.arange(ROPE_FREQS, dtype=np.float64) / ROPE_FREQS)
    coord = np.where((axis == 0)[None, :], (pos // GRID_W)[:, None], (pos % GRID_W)[:, None]).astype(np.float64)
    ang = coord * freqs[lane % ROPE_FREQS][None, :]
    sign = np.where(half == 0, -1.0, 1.0)[None, :]
    return jnp.asarray(np.cos(ang), F32), jnp.asarray(np.sin(ang) * sign, F32)


def _layer(x, c, ctx, c_ctx, ada_w, ada_b, norm_g, w_in, conv_w, conv_b, gate_b, mh_norm_g, q_norm_g, k_norm_g,
           kv_norm_g, w_uk, w_uv, w_proj_m, w_proj_a, w_out):
    b, t, _ = x.shape
    tc = ctx.shape[1]
    assert tc == MLSTM_CHUNK and t % MLSTM_CHUNK == 0 and t % GRID_W == 0

    c8 = jnp.zeros((8, D_MODEL), F32).at[:b].set(c).at[b].set(c_ctx)
    mod = _adaln(c8, ada_w, ada_b)
    shift, scale, gate = mod[:, :D_MODEL], mod[:, D_MODEL:2 * D_MODEL], mod[:, 2 * D_MODEL:]
    per_b = lambda a: a[:b].reshape(b, 1, D_MODEL)
    per_c = lambda a: jnp.broadcast_to(a[b].reshape(1, 1, D_MODEL), (b, 1, D_MODEL))

    w_main, w_ctx, w_small = _layout_w_in(w_in)
    proj, small = _proj(x, per_b(scale), per_b(shift), norm_g, w_main, w_small, P_COLS, 2560)
    proj_c, small_c = _proj(ctx, per_c(scale), per_c(shift), norm_g, w_ctx, w_small, C_COLS, C_COLS // 2)

    cw_q, cw_k = conv_w[:, :M_QK_W], conv_w[:, M_QK_W:]
    cb_q, cb_k = conv_b[:M_QK_W], conv_b[M_QK_W:]
    q_m = _conv(proj, P_QM, cw_q, cb_q, M_DQK ** -0.5, False)
    kt_m = _conv(proj, P_KM, cw_k, cb_k, 1.0, True)
    kt_c = _conv(proj_c, C_KM, cw_k, cb_k, 1.0, True)
    gate_bias = jnp.zeros((1, LANES), F32).at[0, :M_GATE_W].set(gate_b)
    hf, hb = _mlstm(q_m, kt_m, proj, small, kt_c, proj_c, small_c, gate_bias)

    cos_t, sin_t = _rope_tables(t)
    k_a, v_a, q_a = _mla_prep(proj, small, proj_c, small_c, w_uk, w_uv, kv_norm_g, k_norm_g, q_norm_g, cos_t, sin_t)
    o_a = _attention(q_a, k_a, v_a, tc)

    y = _merge(hf, hb, proj, o_a, mh_norm_g, w_proj_m.astype(BF16), w_proj_a.astype(BF16))
    return _out(y, x, per_b(gate), w_out.astype(BF16))


def kernel(x, c, ctx, c_ctx, ada_w, ada_b, norm_g, w_in, conv_w, conv_b, gate_b, mh_norm_g, q_norm_g, k_norm_g,
           kv_norm_g, w_uk, w_uv, w_proj_m, w_proj_a, w_out):
    assert ada_w.shape[0] == 1, "single-layer block"
    return _layer(x, c, ctx, c_ctx, ada_w[0], ada_b[0], norm_g[0], w_in[0], conv_w[0], conv_b[0], gate_b[0],
                  mh_norm_g[0], q_norm_g[0], k_norm_g[0], kv_norm_g[0], w_uk[0], w_uv[0], w_proj_m[0],
                  w_proj_a[0], w_out[0])
```

```python
import functools

import numpy as np
import jax
import jax.numpy as jnp
from jax import lax
from jax.experimental import pallas as pl
from jax.experimental.pallas import tpu as pltpu

F32 = jnp.float32
BF16 = jnp.bfloat16

D_MODEL = 2048
GRID_W = 64
EPS = 1e-6

M_HEADS = 8
M_DQK = 128
M_DV = 256
M_CONV = 5
A_HEADS = 16
A_NOPE = 128
A_ROPE = 64
A_QK = A_NOPE + A_ROPE
A_DV = 128
A_VX = 256
A_QKX = 256
KV_RANK = 512
ROPE_FREQS = A_ROPE // 4
ROPE_THETA = 10000.0

M_QK_W = M_HEADS * M_DQK
M_V_W = M_HEADS * M_DV
M_GATE_W = 4 * M_HEADS
A_Q_W = A_HEADS * A_QK
A_V_W = A_HEADS * A_DV

_O_KM = 0
_O_VM = _O_KM + M_QK_W
_O_GT = _O_VM + M_V_W
_O_CKV = _O_GT + M_GATE_W
_O_KR = _O_CKV + KV_RANK
_O_QM = _O_KR + A_ROPE
_O_OM = _O_QM + M_QK_W
_O_ZM = _O_OM + M_V_W
_O_QA = _O_ZM + M_V_W
_O_ZA = _O_QA + A_Q_W
_O_GM = _O_ZA + A_V_W
_O_END = _O_GM + 2 * D_MODEL

P_VM = 0
P_OM = P_VM + M_V_W
P_ZM = P_OM + M_V_W
P_ZA = P_ZM + M_V_W
P_GM = P_ZA + A_V_W
P_QA = P_GM + 2 * D_MODEL
P_KM = P_QA + A_Q_W
P_QM = P_KM + M_QK_W
P_CKV = P_QM + M_QK_W
P_COLS = P_CKV + KV_RANK
C_VM = 0
C_KM = C_VM + M_V_W
C_CKV = C_KM + M_QK_W
C_COLS = C_CKV + KV_RANK
S_GT = 0
S_KR = 128
S_COLS = 256

LANES = 128
MLSTM_CHUNK = 256
NEG_BIG = -1e30
LOG2_E = 1.4426950408889634
VMEM_LIMIT = 60 * 1024 * 1024


def _cparams(sem):
    return pltpu.CompilerParams(dimension_semantics=sem, vmem_limit_bytes=VMEM_LIMIT)


def _sigmoid(a):
    return 0.5 * jnp.tanh(0.5 * a) + 0.5


def _silu(a):
    half = 0.5 * a
    return half * jnp.tanh(half) + half


def _adaln_kernel(c_ref, w_ref, b_ref, o_ref):
    s = _silu(c_ref[...])
    o_ref[...] = jnp.dot(s.astype(BF16), w_ref[...].astype(BF16), preferred_element_type=F32) + b_ref[...]


def _adaln(c8, ada_w, ada_b):
    n = ada_w.shape[1]
    tn = 1024
    return pl.pallas_call(
        _adaln_kernel,
        grid=(n // tn,),
        in_specs=[pl.BlockSpec((8, D_MODEL), lambda j: (0, 0)),
                  pl.BlockSpec((D_MODEL, tn), lambda j: (0, j)),
                  pl.BlockSpec((1, tn), lambda j: (0, j))],
        out_specs=pl.BlockSpec((8, tn), lambda j: (0, j)),
        out_shape=jax.ShapeDtypeStruct((8, n), F32),
        compiler_params=_cparams(("parallel",)),
        name="adaln",
    )(c8, ada_w, ada_b.reshape(1, n))


def _proj_kernel(x_ref, sc_ref, sh_ref, g_ref, w_ref, ws_ref, o_ref, os_ref, h_ref):
    @pl.when(pl.program_id(2) == 0)
    def _():
        x = x_ref[...]
        ms = jnp.mean(x * x, axis=-1, keepdims=True)
        y = x * lax.rsqrt(ms + EPS) * g_ref[...]
        h = (y * (1.0 + sc_ref[...]) + sh_ref[...]).astype(BF16)
        h_ref[...] = h
        os_ref[...] = jnp.dot(h, ws_ref[...], preferred_element_type=F32)

    o_ref[...] = jnp.dot(h_ref[...], w_ref[...], preferred_element_type=F32).astype(o_ref.dtype)


def _proj(x, scale, shift, norm_g, w_main, w_small, n_cols, tn):
    b, t, _ = x.shape
    tm = min(1024, t)
    return pl.pallas_call(
        _proj_kernel,
        grid=(b, t // tm, n_cols // tn),
        in_specs=[pl.BlockSpec((None, tm, D_MODEL), lambda bi, i, j: (bi, i, 0)),
                  pl.BlockSpec((None, 1, D_MODEL), lambda bi, i, j: (bi, 0, 0)),
                  pl.BlockSpec((None, 1, D_MODEL), lambda bi, i, j: (bi, 0, 0)),
                  pl.BlockSpec((1, D_MODEL), lambda bi, i, j: (0, 0)),
                  pl.BlockSpec((D_MODEL, tn), lambda bi, i, j: (0, j)),
                  pl.BlockSpec((D_MODEL, S_COLS), lambda bi, i, j: (0, 0))],
        out_specs=[pl.BlockSpec((None, tm, tn), lambda bi, i, j: (bi, i, j)),
                   pl.BlockSpec((None, tm, S_COLS), lambda bi, i, j: (bi, i, 0))],
        out_shape=[jax.ShapeDtypeStruct((b, t, n_cols), BF16),
                   jax.ShapeDtypeStruct((b, t, S_COLS), F32)],
        scratch_shapes=[pltpu.VMEM((tm, D_MODEL), BF16)],
        compiler_params=_cparams(("parallel", "parallel", "arbitrary")),
        name="proj",
    )(x, scale, shift, norm_g.reshape(1, D_MODEL), w_main, w_small)


CONV_HALO = 16


def _conv_kernel(prev_ref, cur_ref, next_ref, w_ref, b_ref, o_ref, buf_ref, *, tt, nt, out_scale, transpose):
    i = pl.program_id(1)
    buf_ref[CONV_HALO:CONV_HALO + tt, :] = cur_ref[...].astype(F32)
    buf_ref[0:CONV_HALO, :] = prev_ref[...].astype(F32) * (i > 0).astype(F32)
    buf_ref[CONV_HALO + tt:2 * CONV_HALO + tt, :] = next_ref[...].astype(F32) * (i < nt - 1).astype(F32)
    acc = jnp.broadcast_to(b_ref[...], (tt, cur_ref.shape[-1]))
    for k in range(M_CONV):
        lo = CONV_HALO - M_CONV // 2 + k
        acc = acc + w_ref[k:k + 1, :] * buf_ref[lo:lo + tt, :]
    y = _silu(acc)
    if out_scale != 1.0:
        y = y * out_scale
    o_ref[...] = (y.T if transpose else y).astype(o_ref.dtype)


def _conv(src, col_off, w, bias, out_scale, transpose):
    b, t, _ = src.shape
    cw = 512
    tt = min(512, t)
    nt = t // tt
    cb = col_off // cw
    hb = tt // CONV_HALO
    nhalo = t // CONV_HALO
    if transpose:
        out_shape = jax.ShapeDtypeStruct((b, M_QK_W, t), BF16)
        out_specs = pl.BlockSpec((None, cw, tt), lambda bi, i, c: (bi, c, i))
    else:
        out_shape = jax.ShapeDtypeStruct((b, t, M_QK_W), BF16)
        out_specs = pl.BlockSpec((None, tt, cw), lambda bi, i, c: (bi, i, c))
    return pl.pallas_call(
        functools.partial(_conv_kernel, tt=tt, nt=nt, out_scale=out_scale, transpose=transpose),
        grid=(b, nt, M_QK_W // cw),
        in_specs=[pl.BlockSpec((None, CONV_HALO, cw), lambda bi, i, c: (bi, jnp.maximum(i * hb - 1, 0), cb + c)),
                  pl.BlockSpec((None, tt, cw), lambda bi, i, c: (bi, i, cb + c)),
                  pl.BlockSpec((None, CONV_HALO, cw), lambda bi, i, c: (bi, jnp.minimum((i + 1) * hb, nhalo - 1), cb + c)),
                  pl.BlockSpec((M_CONV, cw), lambda bi, i, c: (0, c)),
                  pl.BlockSpec((1, cw), lambda bi, i, c: (0, c))],
        out_specs=out_specs,
        out_shape=out_shape,
        scratch_shapes=[pltpu.VMEM((tt + 2 * CONV_HALO, cw), F32)],
        compiler_params=_cparams(("parallel", "parallel", "parallel")),
        name="conv_t" if transpose else "conv",
    )(src, src, src, w, bias.reshape(1, M_QK_W))


def _mlstm_unit(q, kt, v, g_col, g_row, ig_row, b_tot, mask, ones, c_ref, n_ref, m_ref, idx):
    m_prev = m_ref[idx][0:1, 0:1]
    c_prev = c_ref[idx]
    n_prev = n_ref[idx]
    a_row = ig_row - g_row
    h_out = None
    if q is not None:
        am = jnp.where(mask, a_row, NEG_BIG)
        c_col = jnp.maximum(m_prev, jnp.max(am, axis=-1, keepdims=True))
        c_b = jnp.broadcast_to(c_col, mask.shape)
        s = (jnp.dot(q, kt, preferred_element_type=F32) * jnp.exp2(am - c_b)).astype(BF16)
        w_state = jnp.exp2(m_prev - c_b)
        num = (jnp.dot(s, v, preferred_element_type=F32)
               + w_state * jnp.dot(q, c_prev.astype(BF16), preferred_element_type=F32))
        den = (jnp.dot(s, ones, preferred_element_type=F32)
               + w_state[:, :LANES] * jnp.dot(q, n_prev.astype(BF16), preferred_element_type=F32))
        den = jnp.maximum(jnp.abs(den), jnp.exp2(-(g_col + c_col)))
        h_out = num / jnp.concatenate([den] * (num.shape[-1] // LANES), axis=-1)
    w_row = b_tot + a_row
    m_new = jnp.maximum(b_tot + m_prev, jnp.max(w_row, axis=-1, keepdims=True))
    decay = jnp.exp2(b_tot + m_prev - m_new)
    kw = kt * jnp.exp2(w_row - m_new).astype(BF16)
    c_ref[idx] = decay * c_prev + jnp.dot(kw, v, preferred_element_type=F32)
    n_ref[idx] = decay * n_prev + jnp.dot(kw, ones, preferred_element_type=F32)
    m_ref[idx] = jnp.broadcast_to(m_new, m_ref.shape[1:])
    return h_out


def _mlstm_gates(gt_ref, gb_ref, tri):
    a = gt_ref[...] + gb_ref[...]
    lane = lax.broadcasted_iota(jnp.int32, a.shape, 1)
    is_forget = ((lane // M_HEADS) % 2) == 1
    act = jnp.where(is_forget, jax.nn.log_sigmoid(a), a) * LOG2_E
    cum = jnp.dot(tri, act, preferred_element_type=F32, precision=lax.Precision.HIGHEST)
    return cum, act.T, cum.T


def _mlstm_kernel(qf_ref, ktf_ref, vf_ref, gf_ref,
                  qb_ref, ktb_ref, vb_ref, gb_ref,
                  ktc_ref, vc_ref, gc_ref, gbias_ref,
                  hf_ref, hb_ref, c_ref, n_ref, m_ref):
    i = pl.program_id(1)
    L = MLSTM_CHUNK
    rows = lax.broadcasted_iota(jnp.int32, (L, L), 0)
    cols = lax.broadcasted_iota(jnp.int32, (L, L), 1)
    lower = cols <= rows
    upper = cols >= rows
    ones = jnp.ones((L, LANES), BF16)

    def run(q_ref, kt_ref, v_ref, g_ref, h_ref, direction):
        causal = direction == 0
        mask = lower if causal else upper
        cum, act_t, cum_t = _mlstm_gates(g_ref, gbias_ref, mask.astype(F32))
        for hd in range(M_HEADS):
            ci = 2 * M_HEADS * direction + hd
            cf = ci + M_HEADS
            g_col = cum[:, cf:cf + 1]
            b_tot = g_col[L - 1:L, :] if causal else g_col[0:1, :]
            ks = slice(hd * M_DQK, (hd + 1) * M_DQK)
            vs = slice(hd * M_DV, (hd + 1) * M_DV)
            q = None if q_ref is None else q_ref[:, ks]
            h = _mlstm_unit(q, kt_ref[ks, :], v_ref[:, vs], g_col, cum_t[cf:cf + 1, :], act_t[ci:ci + 1, :],
                            b_tot, mask, ones, c_ref, n_ref, m_ref, direction * M_HEADS + hd)
            if h is not None:
                h_ref[:, vs] = h.astype(h_ref.dtype)

    @pl.when(i == 0)
    def _():
        c_ref[...] = jnp.zeros_like(c_ref)
        n_ref[...] = jnp.zeros_like(n_ref)
        m_ref[...] = jnp.zeros_like(m_ref)
        run(None, ktc_ref, vc_ref, gc_ref, None, 0)
        run(None, ktc_ref, vc_ref, gc_ref, None, 1)

    @pl.when(i > 0)
    def _():
        run(qf_ref, ktf_ref, vf_ref, gf_ref, hf_ref, 0)
        run(qb_ref, ktb_ref, vb_ref, gb_ref, hb_ref, 1)


def _mlstm(q, kt, proj, small, ktc, proj_c, small_c, gate_bias):
    b, t, _ = q.shape
    L = MLSTM_CHUNK
    nc = t // L
    fwd = lambda bi, i: (bi, jnp.maximum(i - 1, 0), 0)
    bwd = lambda bi, i: (bi, jnp.minimum(nc - i, nc - 1), 0)
    fwd_t = lambda bi, i: (bi, 0, jnp.maximum(i - 1, 0))
    bwd_t = lambda bi, i: (bi, 0, jnp.minimum(nc - i, nc - 1))
    ctx = lambda bi, i: (bi, 0, 0)

    def specs(rm, tm_):
        return [pl.BlockSpec((None, L, M_QK_W), rm), pl.BlockSpec((None, M_QK_W, L), tm_),
                pl.BlockSpec((None, L, M_V_W), rm), pl.BlockSpec((None, L, LANES), rm)]

    in_specs = (specs(fwd, fwd_t) + specs(bwd, bwd_t)
                + [pl.BlockSpec((None, M_QK_W, L), ctx), pl.BlockSpec((None, L, M_V_W), ctx),
                   pl.BlockSpec((None, L, LANES), ctx), pl.BlockSpec((1, LANES), lambda bi, i: (0, 0))])
    return pl.pallas_call(
        _mlstm_kernel,
        grid=(b, nc + 1),
        in_specs=in_specs,
        out_specs=[pl.BlockSpec((None, L, M_V_W), fwd), pl.BlockSpec((None, L, M_V_W), bwd)],
        out_shape=[jax.ShapeDtypeStruct((b, t, M_V_W), BF16)] * 2,
        scratch_shapes=[pltpu.VMEM((2 * M_HEADS, M_DQK, M_DV), F32),
                        pltpu.VMEM((2 * M_HEADS, M_DQK, LANES), F32),
                        pltpu.VMEM((2 * M_HEADS, 8, LANES), F32)],
        compiler_params=_cparams(("parallel", "arbitrary")),
        name="mlstm",
    )(q, kt, proj, small, q, kt, proj, small, ktc, proj_c, small_c, gate_bias)


def _rope_partner_index():
    lane = np.arange(LANES)
    return np.where(lane % (2 * ROPE_FREQS) < ROPE_FREQS, lane + ROPE_FREQS, lane - ROPE_FREQS)


def _mla_prep_kernel(ckv_x_ref, sm_x_ref, ckv_c_ref, sm_c_ref, qa_ref,
                     wuk_ref, wuv_ref, kvg_ref, kgn_ref, kgr_ref, qgw_ref, qgs_ref, cos_ref, sin_ref,
                     pair_ref, sel_ref, half_ref, swap_ref, onecol_ref,
                     k_ref, v_ref, q_ref):
    i = pl.program_id(1)
    rows = ckv_x_ref.shape[0]
    lane_id = lax.broadcasted_iota(jnp.int32, (rows, LANES), 1)
    low_half = lane_id < A_ROPE
    low_half_of_pair = (lane_id % (2 * ROPE_FREQS)) < ROPE_FREQS
    eps_dim = A_QK * EPS

    def keys_values(ckv_ref, sm_ref, rotate):
        ckv = ckv_ref[...].astype(F32)
        cn = (ckv * lax.rsqrt(jnp.mean(ckv * ckv, axis=-1, keepdims=True) + EPS) * kvg_ref[...]).astype(BF16)
        v = jnp.dot(cn, wuv_ref[...], preferred_element_type=F32).astype(v_ref.dtype)
        ones_col = onecol_ref[...]
        for hd in range(A_HEADS):
            v_ref[:, hd * A_VX:hd * A_VX + A_DV] = v[:, hd * A_DV:(hd + 1) * A_DV]
            v_ref[:, hd * A_VX + A_DV:(hd + 1) * A_VX] = ones_col
        kn = jnp.dot(cn, wuk_ref[...], preferred_element_type=F32)
        sq = (kn * kn).astype(BF16)
        pair = pair_ref[...]
        ss = jnp.concatenate([jnp.dot(sq[:, g * 2 * A_NOPE:(g + 1) * 2 * A_NOPE], pair, preferred_element_type=F32)
                              for g in range(A_HEADS // 2)], axis=-1)
        kr2 = sm_ref[:, S_KR:S_KR + LANES]
        kr_ss = jnp.dot((kr2 * kr2).astype(BF16), half_ref[...], preferred_element_type=F32)
        r = lax.rsqrt(ss + jnp.concatenate([kr_ss + eps_dim] * (A_HEADS // 2), axis=-1))
        kns = kn * r * kgn_ref[...]
        krg = kr2 * kgr_ref[...]
        if rotate:
            partner = jnp.where(low_half_of_pair, pltpu.roll(krg, LANES - ROPE_FREQS, 1), pltpu.roll(krg, ROPE_FREQS, 1))
            krg = krg * cos_ref[...] + partner * sin_ref[...]
        for hd in range(A_HEADS):
            hs = slice(hd * A_NOPE, (hd + 1) * A_NOPE)
            kn_h = kns[:, hs]
            kr_h = krg * r[:, hs]
            if hd % 2 == 0:
                k_ref[hd, :, :LANES] = kn_h.astype(k_ref.dtype)
                k_ref[hd, :, LANES:] = jnp.where(low_half, kr_h, 0.0).astype(k_ref.dtype)
            else:
                k_ref[hd, :, :LANES] = jnp.where(low_half, 0.0, kn_h).astype(k_ref.dtype)
                k_ref[hd, :, LANES:] = jnp.where(low_half, kn_h, kr_h).astype(k_ref.dtype)

    @pl.when(i == 0)
    def _():
        keys_values(ckv_c_ref, sm_c_ref, False)

    @pl.when(i > 0)
    def _():
        keys_values(ckv_x_ref, sm_x_ref, True)
        cos_e, sin_e = cos_ref[...], sin_ref[...]
        cos_o, sin_o = jnp.where(low_half, 1.0, cos_e), jnp.where(low_half, 0.0, sin_e)
        swap = swap_ref[...]
        for hd in range(A_HEADS):
            odd = hd % 2
            lo = (hd // 2) * 3 * LANES + odd * LANES
            win = qa_ref[:, lo:lo + A_QKX]
            wf = win.astype(F32)
            ss = jnp.dot((wf * wf).astype(BF16), sel_ref[odd], preferred_element_type=F32)
            r = lax.rsqrt(ss + eps_dim)
            x1 = wf[:, :LANES] * qgw_ref[odd:odd + 1, :LANES]
            x2 = wf[:, LANES:] * qgw_ref[odd:odd + 1, LANES:]
            p2 = jnp.dot(win[:, LANES:], swap, preferred_element_type=F32) * qgs_ref[odd:odd + 1, :]
            rot2 = x2 * (cos_o if odd else cos_e) + p2 * (sin_o if odd else sin_e)
            q_ref[hd, :, :LANES] = (x1 * r).astype(q_ref.dtype)
            q_ref[hd, :, LANES:] = (rot2 * r).astype(q_ref.dtype)


def _mla_prep(proj, small, proj_c, small_c, w_uk, w_uv, kv_norm_g, k_norm_g, q_norm_g, cos_t, sin_t):
    b, t, _ = proj.shape
    tc = proj_c.shape[1]
    tr = tc
    nx = t // tr
    xrow = lambda bi, i: (bi, jnp.maximum(i - 1, 0), 0)
    const = lambda *shape: pl.BlockSpec(shape, lambda bi, i: (0,) * len(shape))

    halves = lambda a: a.reshape(a.shape[:-1] + (A_HEADS // 2, 2, 2, A_NOPE // 2))
    swap_odd = lambda a: jnp.concatenate([halves(a)[..., 0:1, :, :], halves(a)[..., 1:2, ::-1, :]],
                                         axis=-3).reshape(a.shape)
    w_uk_l = swap_odd(w_uk)
    gk_n, gk_r = k_norm_g[:A_NOPE] * A_QK ** 0.5, k_norm_g[A_NOPE:] * A_QK ** 0.5
    gq_n, gq_r = q_norm_g[:A_NOPE] * LOG2_E, q_norm_g[A_NOPE:] * LOG2_E
    kgn = swap_odd(jnp.tile(gk_n, A_HEADS)).reshape(1, A_HEADS * A_NOPE)
    kgr = jnp.tile(gk_r, 2).reshape(1, LANES)
    zeros = jnp.zeros((A_ROPE,), F32)
    qgw = jnp.stack([jnp.concatenate([gq_n, gq_r, zeros]), jnp.concatenate([zeros, gq_n, gq_r])])
    partner = _rope_partner_index()
    qgs = qgw[:, LANES:][:, partner]

    lane = np.arange(A_QKX)
    pair = (lane[:, None] // A_NOPE == lane[None, :] // A_NOPE).astype(np.float32)
    sel = np.stack([np.broadcast_to((lane < A_QK)[:, None], (A_QKX, LANES)),
                    np.broadcast_to((lane >= A_QKX - A_QK)[:, None], (A_QKX, LANES))]).astype(np.float32)
    half = np.broadcast_to((np.arange(LANES) < A_ROPE)[:, None], (LANES, A_QKX)).astype(np.float32)
    swap = np.zeros((LANES, LANES), np.float32)
    swap[partner, np.arange(LANES)] = 1.0
    onecol = np.zeros((tr, A_VX - A_DV), np.float32)
    onecol[:, 0] = 1.0
    as_bf16 = lambda a: jnp.asarray(a, BF16)

    return pl.pallas_call(
        _mla_prep_kernel,
        grid=(b, nx + 1),
        in_specs=[pl.BlockSpec((None, tr, KV_RANK), lambda bi, i: (bi, jnp.maximum(i - 1, 0), P_CKV // KV_RANK)),
                  pl.BlockSpec((None, tr, S_COLS), xrow),
                  pl.BlockSpec((None, tr, KV_RANK), lambda bi, i: (bi, 0, C_CKV // KV_RANK)),
                  pl.BlockSpec((None, tr, S_COLS), lambda bi, i: (bi, 0, 0)),
                  pl.BlockSpec((None, tr, A_Q_W), lambda bi, i: (bi, jnp.maximum(i - 1, 0), P_QA // A_Q_W)),
                  const(KV_RANK, A_HEADS * A_NOPE), const(KV_RANK, A_V_W), const(1, KV_RANK),
                  const(1, A_HEADS * A_NOPE), const(1, LANES), const(2, A_QKX), const(2, LANES),
                  pl.BlockSpec((tr, LANES), lambda bi, i: (jnp.maximum(i - 1, 0), 0)),
                  pl.BlockSpec((tr, LANES), lambda bi, i: (jnp.maximum(i - 1, 0), 0)),
                  const(A_QKX, A_QKX), const(2, A_QKX, LANES), const(LANES, A_QKX), const(LANES, LANES),
                  const(tr, A_VX - A_DV)],
        out_specs=[pl.BlockSpec((None, A_HEADS, tr, A_QKX), lambda bi, i: (bi, 0, i, 0)),
                   pl.BlockSpec((None, tr, A_HEADS * A_VX), lambda bi, i: (bi, i, 0)),
                   pl.BlockSpec((None, A_HEADS, tr, A_QKX), lambda bi, i: (bi, 0, jnp.maximum(i - 1, 0), 0))],
        out_shape=[jax.ShapeDtypeStruct((b, A_HEADS, tc + t, A_QKX), BF16),
                   jax.ShapeDtypeStruct((b, tc + t, A_HEADS * A_VX), BF16),
                   jax.ShapeDtypeStruct((b, A_HEADS, t, A_QKX), BF16)],
        compiler_params=_cparams(("parallel", "arbitrary")),
        name="mla_prep",
    )(proj, small, proj_c, small_c, proj, w_uk_l.astype(BF16), w_uv.astype(BF16), kv_norm_g.reshape(1, KV_RANK),
      kgn, kgr, qgw, qgs, cos_t, sin_t, as_bf16(pair), as_bf16(sel), as_bf16(half), as_bf16(swap), as_bf16(onecol))


def _attn_kernel(q_ref, k_ref, v_ref, o_ref, *, key_blocks):
    q = q_ref[...]
    m = acc = None
    for lo, hi in key_blocks:
        s = lax.dot_general(q, k_ref[lo:hi, :], (((1,), (1,)), ((), ())), preferred_element_type=F32)
        m_blk = jnp.max(s, axis=-1, keepdims=True)
        m_new = m_blk if m is None else jnp.maximum(m, m_blk)
        p = jnp.exp2(s - m_new).astype(BF16)
        pv = jnp.dot(p, v_ref[lo:hi, :], preferred_element_type=F32)
        acc = pv if acc is None else acc * jnp.exp2(m - m_new) + pv
        m = m_new
    o_ref[...] = (acc[:, :A_DV] / acc[:, A_DV:A_DV + 1]).astype(o_ref.dtype)


ATTN_KEY_BLOCK = 256


def _attention(q, k, v, n_ctx):
    b, h, s, _ = q.shape
    tk = k.shape[2]
    tq = min(1024, s)
    key_blocks = [(lo, lo + ATTN_KEY_BLOCK) for lo in range(n_ctx, tk, ATTN_KEY_BLOCK)] + [(0, n_ctx)]
    return pl.pallas_call(
        functools.partial(_attn_kernel, key_blocks=tuple(key_blocks)),
        grid=(b, h, s // tq),
        in_specs=[pl.BlockSpec((None, None, tq, A_QKX), lambda bi, hi, i: (bi, hi, i, 0)),
                  pl.BlockSpec((None, None, tk, A_QKX), lambda bi, hi, i: (bi, hi, 0, 0)),
                  pl.BlockSpec((None, tk, A_VX), lambda bi, hi, i: (bi, 0, hi))],
        out_specs=pl.BlockSpec((None, tq, A_DV), lambda bi, hi, i: (bi, i, hi)),
        out_shape=jax.ShapeDtypeStruct((b, s, A_V_W), BF16),
        compiler_params=_cparams(("parallel", "parallel", "parallel")),
        name="attention",
    )(q, k, v)


def _merge_kernel(hf_ref, hb_ref, om_ref, zm_ref, oa_ref, za_ref, gm_ref, ga_ref, mhg_ref,
                  wm_ref, wa_ref, o_ref, hm_ref):
    for hd in range(M_HEADS):
        vs = slice(hd * M_DV, (hd + 1) * M_DV)
        h = hf_ref[:, vs].astype(F32) + hb_ref[:, vs].astype(F32)
        hn = h * lax.rsqrt(jnp.mean(h * h, axis=-1, keepdims=True) + EPS) * mhg_ref[:, vs]
        gated = hn * _sigmoid(om_ref[:, vs].astype(F32)) * _silu(zm_ref[:, vs].astype(F32))
        hm_ref[:, vs] = gated.astype(BF16)
    p_m = jnp.dot(hm_ref[...], wm_ref[...], preferred_element_type=F32)
    oa = (oa_ref[...].astype(F32) * _silu(za_ref[...].astype(F32))).astype(BF16)
    p_a = jnp.dot(oa, wa_ref[...], preferred_element_type=F32)
    y = _sigmoid(gm_ref[...].astype(F32)) * p_m + _sigmoid(ga_ref[...].astype(F32)) * p_a
    o_ref[...] = y.astype(o_ref.dtype)


def _merge(hf, hb, proj, oa, mh_norm_g, w_proj_m, w_proj_a):
    b, t, _ = hf.shape
    tm = 256
    row = lambda bi, i: (bi, i, 0)
    col = lambda c: (lambda bi, i: (bi, i, c))
    wspec = pl.BlockSpec((D_MODEL, D_MODEL), lambda bi, i: (0, 0))
    act = lambda im: pl.BlockSpec((None, tm, D_MODEL), im)
    return pl.pallas_call(
        _merge_kernel,
        grid=(b, t // tm),
        in_specs=[act(row), act(row),
                  act(col(P_OM // D_MODEL)), act(col(P_ZM // D_MODEL)),
                  act(row), act(col(P_ZA // D_MODEL)),
                  act(col(P_GM // D_MODEL)), act(col(P_GM // D_MODEL + 1)),
                  pl.BlockSpec((1, M_V_W), lambda bi, i: (0, 0)),
                  wspec, wspec],
        out_specs=act(row),
        out_shape=jax.ShapeDtypeStruct((b, t, D_MODEL), BF16),
        scratch_shapes=[pltpu.VMEM((tm, M_V_W), BF16)],
        compiler_params=_cparams(("parallel", "parallel")),
        name="merge",
    )(hf, hb, proj, proj, oa, proj, proj, proj, mh_norm_g.reshape(1, M_V_W), w_proj_m, w_proj_a)


def _out_kernel(y_ref, x_ref, gate_ref, w_ref, o_ref):
    o_ref[...] = x_ref[...] + gate_ref[...] * jnp.dot(y_ref[...], w_ref[...], preferred_element_type=F32)


def _out(y, x, gate, w_out):
    b, t, _ = x.shape
    tm = min(512, t)
    row = lambda bi, i: (bi, i, 0)
    return pl.pallas_call(
        _out_kernel,
        grid=(b, t // tm),
        in_specs=[pl.BlockSpec((None, tm, D_MODEL), row),
                  pl.BlockSpec((None, tm, D_MODEL), row),
                  pl.BlockSpec((None, 1, D_MODEL), lambda bi, i: (bi, 0, 0)),
                  pl.BlockSpec((D_MODEL, D_MODEL), lambda bi, i: (0, 0))],
        out_specs=pl.BlockSpec((None, tm, D_MODEL), row),
        out_shape=jax.ShapeDtypeStruct((b, t, D_MODEL), F32),
        compiler_params=_cparams(("parallel", "parallel")),
        name="out",
    )(y, x, gate, w_out)


def _layout_w_in(w):
    km, vm, ckv, kr = w[:, _O_KM:_O_VM], w[:, _O_VM:_O_GT], w[:, _O_CKV:_O_KR], w[:, _O_KR:_O_QM]
    main = jnp.concatenate([vm, w[:, _O_OM:_O_ZM], w[:, _O_ZM:_O_QA], w[:, _O_ZA:_O_GM], w[:, _O_GM:_O_END],
                            w[:, _O_QA:_O_ZA], km, w[:, _O_QM:_O_OM], ckv], axis=1).astype(BF16)
    ctx = jnp.concatenate([vm, km, ckv], axis=1).astype(BF16)
    small = jnp.concatenate([w[:, _O_GT:_O_CKV], jnp.zeros((D_MODEL, S_KR - M_GATE_W), w.dtype), kr, kr],
                            axis=1).astype(BF16)
    return main, ctx, small


def _rope_tables(seq):
    pos = np.arange(seq)
    lane = np.arange(LANES) % A_ROPE
    axis = lane // (2 * ROPE_FREQS)
    half = (lane % (2 * ROPE_FREQS)) // ROPE_FREQS
    freqs = ROPE_THETA ** (-np.arange(ROPE_FREQS, dtype=np.float64) / ROPE_FREQS)
    coord = np.where((axis == 0)[None, :], (pos // GRID_W)[:, None], (pos % GRID_W)[:, None]).astype(np.float64)
    ang = coord * freqs[lane % ROPE_FREQS][None, :]
    sign = np.where(half == 0, -1.0, 1.0)[None, :]
    return jnp.asarray(np.cos(ang), F32), jnp.asarray(np.sin(ang) * sign, F32)


def _layer(x, c, ctx, c_ctx, ada_w, ada_b, norm_g, w_in, conv_w, conv_b, gate_b, mh_norm_g, q_norm_g, k_norm_g,
           kv_norm_g, w_uk, w_uv, w_proj_m, w_proj_a, w_out):
    b, t, _ = x.shape
    tc = ctx.shape[1]
    assert tc == MLSTM_CHUNK and t % MLSTM_CHUNK == 0 and t % GRID_W == 0

    c8 = jnp.zeros((8, D_MODEL), F32).at[:b].set(c).at[b].set(c_ctx)
    mod = _adaln(c8, ada_w, ada_b)
    shift, scale, gate = mod[:, :D_MODEL], mod[:, D_MODEL:2 * D_MODEL], mod[:, 2 * D_MODEL:]
    per_b = lambda a: a[:b].reshape(b, 1, D_MODEL)
    per_c = lambda a: jnp.broadcast_to(a[b].reshape(1, 1, D_MODEL), (b, 1, D_MODEL))

    w_main, w_ctx, w_small = _layout_w_in(w_in)
    proj, small = _proj(x, per_b(scale), per_b(shift), norm_g, w_main, w_small, P_COLS, 2560)
    proj_c, small_c = _proj(ctx, per_c(scale), per_c(shift), norm_g, w_ctx, w_small, C_COLS, C_COLS // 2)

    cw_q, cw_k = conv_w[:, :M_QK_W], conv_w[:, M_QK_W:]
    cb_q, cb_k = conv_b[:M_QK_W], conv_b[M_QK_W:]
    q_m = _conv(proj, P_QM, cw_q, cb_q, M_DQK ** -0.5, False)
    kt_m = _conv(proj, P_KM, cw_k, cb_k, 1.0, True)
    kt_c = _conv(proj_c, C_KM, cw_k, cb_k, 1.0, True)
    gate_bias = jnp.zeros((1, LANES), F32).at[0, :M_GATE_W].set(gate_b)
    hf, hb = _mlstm(q_m, kt_m, proj, small, kt_c, proj_c, small_c, gate_bias)

    cos_t, sin_t = _rope_tables(t)
    k_a, v_a, q_a = _mla_prep(proj, small, proj_c, small_c, w_uk, w_uv, kv_norm_g, k_norm_g, q_norm_g, cos_t, sin_t)
    o_a = _attention(q_a, k_a, v_a, tc)

    y = _merge(hf, hb, proj, o_a, mh_norm_g, w_proj_m.astype(BF16), w_proj_a.astype(BF16))
    return _out(y, x, per_b(gate), w_out.astype(BF16))


def kernel(x, c, ctx, c_ctx, ada_w, ada_b, norm_g, w_in, conv_w, conv_b, gate_b, mh_norm_g, q_norm_g, k_norm_g,
           kv_norm_g, w_uk, w_uv, w_proj_m, w_proj_a, w_out):
    assert ada_w.shape[0] == 1, "single-layer block"
    return _layer(x, c, ctx, c_ctx, ada_w[0], ada_b[0], norm_g[0], w_in[0], conv_w[0], conv_b[0], gate_b[0],
                  mh_norm_g[0], q_norm_g[0], k_norm_g[0], kv_norm_g[0], w_uk[0], w_uv[0], w_proj_m[0],
                  w_proj_a[0], w_out[0])
```

```python
import functools

import numpy as np
import jax
import jax.numpy as jnp
from jax import lax
from jax.experimental import pallas as pl
from jax.experimental.pallas import tpu as pltpu

F32 = jnp.float32
BF16 = jnp.bfloat16

D_MODEL = 2048
GRID_W = 64
EPS = 1e-6

M_HEADS = 8
M_DQK = 128
M_DV = 256
M_CONV = 5
A_HEADS = 16
A_NOPE = 128
A_ROPE = 64
A_QK = A_NOPE + A_ROPE
A_DV = 128
A_VX = 256
A_QKX = 256
KV_RANK = 512
ROPE_FREQS = A_ROPE // 4
ROPE_THETA = 10000.0

M_QK_W = M_HEADS * M_DQK
M_V_W = M_HEADS * M_DV
M_GATE_W = 4 * M_HEADS
A_Q_W = A_HEADS * A_QK
A_V_W = A_HEADS * A_DV

_O_KM = 0
_O_VM = _O_KM + M_QK_W
_O_GT = _O_VM + M_V_W
_O_CKV = _O_GT + M_GATE_W
_O_KR = _O_CKV + KV_RANK
_O_QM = _O_KR + A_ROPE
_O_OM = _O_QM + M_QK_W
_O_ZM = _O_OM + M_V_W
_O_QA = _O_ZM + M_V_W
_O_ZA = _O_QA + A_Q_W
_O_GM = _O_ZA + A_V_W
_O_END = _O_GM + 2 * D_MODEL

P_VM = 0
P_OM = P_VM + M_V_W
P_ZM = P_OM + M_V_W
P_ZA = P_ZM + M_V_W
P_GM = P_ZA + A_V_W
P_QA = P_GM + 2 * D_MODEL
P_KM = P_QA + A_Q_W
P_QM = P_KM + M_QK_W
P_CKV = P_QM + M_QK_W
P_COLS = P_CKV + KV_RANK
C_VM = 0
C_KM = C_VM + M_V_W
C_CKV = C_KM + M_QK_W
C_COLS = C_CKV + KV_RANK
S_GT = 0
S_KR = 128
S_COLS = 256

LANES = 128
MLSTM_CHUNK = 256
NEG_BIG = -1e30
LOG2_E = 1.4426950408889634
VMEM_LIMIT = 60 * 1024 * 1024


def _cparams(sem):
    return pltpu.CompilerParams(dimension_semantics=sem, vmem_limit_bytes=VMEM_LIMIT)


def _sigmoid(a):
    return 0.5 * jnp.tanh(0.5 * a) + 0.5


def _silu(a):
    half = 0.5 * a
    return half * jnp.tanh(half) + half


def _adaln_kernel(c_ref, w_ref, b_ref, o_ref):
    s = _silu(c_ref[...])
    o_ref[...] = jnp.dot(s.astype(BF16), w_ref[...].astype(BF16), preferred_element_type=F32) + b_ref[...]


def _adaln(c8, ada_w, ada_b):
    n = ada_w.shape[1]
    tn = 1024
    return pl.pallas_call(
        _adaln_kernel,
        grid=(n // tn,),
        in_specs=[pl.BlockSpec((8, D_MODEL), lambda j: (0, 0)),
                  pl.BlockSpec((D_MODEL, tn), lambda j: (0, j)),
                  pl.BlockSpec((1, tn), lambda j: (0, j))],
        out_specs=pl.BlockSpec((8, tn), lambda j: (0, j)),
        out_shape=jax.ShapeDtypeStruct((8, n), F32),
        compiler_params=_cparams(("parallel",)),
        name="adaln",
    )(c8, ada_w, ada_b.reshape(1, n))


def _proj_kernel(x_ref, sc_ref, sh_ref, g_ref, w_ref, ws_ref, o_ref, os_ref, h_ref):
    @pl.when(pl.program_id(2) == 0)
    def _():
        x = x_ref[...]
        ms = jnp.mean(x * x, axis=-1, keepdims=True)
        y = x * lax.rsqrt(ms + EPS) * g_ref[...]
        h = (y * (1.0 + sc_ref[...]) + sh_ref[...]).astype(BF16)
        h_ref[...] = h
        os_ref[...] = jnp.dot(h, ws_ref[...], preferred_element_type=F32)

    o_ref[...] = jnp.dot(h_ref[...], w_ref[...], preferred_element_type=F32).astype(o_ref.dtype)


def _proj(x, scale, shift, norm_g, w_main, w_small, n_cols, tn):
    b, t, _ = x.shape
    tm = min(1024, t)
    return pl.pallas_call(
        _proj_kernel,
        grid=(b, t // tm, n_cols // tn),
        in_specs=[pl.BlockSpec((None, tm, D_MODEL), lambda bi, i, j: (bi, i, 0)),
                  pl.BlockSpec((None, 1, D_MODEL), lambda bi, i, j: (bi, 0, 0)),
                  pl.BlockSpec((None, 1, D_MODEL), lambda bi, i, j: (bi, 0, 0)),
                  pl.BlockSpec((1, D_MODEL), lambda bi, i, j: (0, 0)),
                  pl.BlockSpec((D_MODEL, tn), lambda bi, i, j: (0, j)),
                  pl.BlockSpec((D_MODEL, S_COLS), lambda bi, i, j: (0, 0))],
        out_specs=[pl.BlockSpec((None, tm, tn), lambda bi, i, j: (bi, i, j)),
                   pl.BlockSpec((None, tm, S_COLS), lambda bi, i, j: (bi, i, 0))],
        out_shape=[jax.ShapeDtypeStruct((b, t, n_cols), BF16),
                   jax.ShapeDtypeStruct((b, t, S_COLS), F32)],
        scratch_shapes=[pltpu.VMEM((tm, D_MODEL), BF16)],
        compiler_params=_cparams(("parallel", "parallel", "arbitrary")),
        name="proj",
    )(x, scale, shift, norm_g.reshape(1, D_MODEL), w_main, w_small)


CONV_HALO = 16


def _conv_kernel(prev_ref, cur_ref, next_ref, w_ref, b_ref, o_ref, buf_ref, *, tt, nt, out_scale, transpose):
    i = pl.program_id(1)
    buf_ref[CONV_HALO:CONV_HALO + tt, :] = cur_ref[...].astype(F32)
    buf_ref[0:CONV_HALO, :] = prev_ref[...].astype(F32) * (i > 0).astype(F32)
    buf_ref[CONV_HALO + tt:2 * CONV_HALO + tt, :] = next_ref[...].astype(F32) * (i < nt - 1).astype(F32)
    acc = jnp.broadcast_to(b_ref[...], (tt, cur_ref.shape[-1]))
    for k in range(M_CONV):
        lo = CONV_HALO - M_CONV // 2 + k
        acc = acc + w_ref[k:k + 1, :] * buf_ref[lo:lo + tt, :]
    y = _silu(acc)
    if out_scale != 1.0:
        y = y * out_scale
    o_ref[...] = (y.T if transpose else y).astype(o_ref.dtype)


def _conv(src, col_off, w, bias, out_scale, transpose):
    b, t, _ = src.shape
    cw = 512
    tt = min(512, t)
    nt = t // tt
    cb = col_off // cw
    hb = tt // CONV_HALO
    nhalo = t // CONV_HALO
    if transpose:
        out_shape = jax.ShapeDtypeStruct((b, M_QK_W, t), BF16)
        out_specs = pl.BlockSpec((None, cw, tt), lambda bi, i, c: (bi, c, i))
    else:
        out_shape = jax.ShapeDtypeStruct((b, t, M_QK_W), BF16)
        out_specs = pl.BlockSpec((None, tt, cw), lambda bi, i, c: (bi, i, c))
    return pl.pallas_call(
        functools.partial(_conv_kernel, tt=tt, nt=nt, out_scale=out_scale, transpose=transpose),
        grid=(b, nt, M_QK_W // cw),
        in_specs=[pl.BlockSpec((None, CONV_HALO, cw), lambda bi, i, c: (bi, jnp.maximum(i * hb - 1, 0), cb + c)),
                  pl.BlockSpec((None, tt, cw), lambda bi, i, c: (bi, i, cb + c)),
                  pl.BlockSpec((None, CONV_HALO, cw), lambda bi, i, c: (bi, jnp.minimum((i + 1) * hb, nhalo - 1), cb + c)),
                  pl.BlockSpec((M_CONV, cw), lambda bi, i, c: (0, c)),
                  pl.BlockSpec((1, cw), lambda bi, i, c: (0, c))],
        out_specs=out_specs,
        out_shape=out_shape,
        scratch_shapes=[pltpu.VMEM((tt + 2 * CONV_HALO, cw), F32)],
        compiler_params=_cparams(("parallel", "parallel", "parallel")),
        name="conv_t" if transpose else "conv",
    )(src, src, src, w, bias.reshape(1, M_QK_W))


def _mlstm_unit(q, kt, v, g_col, g_row, ig_row, b_tot, mask, ones, c_ref, n_ref, m_ref, idx):
    m_prev = m_ref[idx][0:1, 0:1]
    c_prev = c_ref[idx]
    n_prev = n_ref[idx]
    a_row = ig_row - g_row
    h_out = None
    if q is not None:
        am = jnp.where(mask, a_row, NEG_BIG)
        c_col = jnp.maximum(m_prev, jnp.max(am, axis=-1, keepdims=True))
        c_b = jnp.broadcast_to(c_col, mask.shape)
        s = (jnp.dot(q, kt, preferred_element_type=F32) * jnp.exp2(am - c_b)).astype(BF16)
        w_state = jnp.exp2(m_prev - c_b)
        num = (jnp.dot(s, v, preferred_element_type=F32)
               + w_state * jnp.dot(q, c_prev.astype(BF16), preferred_element_type=F32))
        den = (jnp.dot(s, ones, preferred_element_type=F32)
               + w_state[:, :LANES] * jnp.dot(q, n_prev.astype(BF16), preferred_element_type=F32))
        den = jnp.maximum(jnp.abs(den), jnp.exp2(-(g_col + c_col)))
        h_out = num / jnp.concatenate([den] * (num.shape[-1] // LANES), axis=-1)
    w_row = b_tot + a_row
    m_new = jnp.maximum(b_tot + m_prev, jnp.max(w_row, axis=-1, keepdims=True))
    decay = jnp.exp2(b_tot + m_prev - m_new)
    kw = kt * jnp.exp2(w_row - m_new).astype(BF16)
    c_ref[idx] = decay * c_prev + jnp.dot(kw, v, preferred_element_type=F32)
    n_ref[idx] = decay * n_prev + jnp.dot(kw, ones, preferred_element_type=F32)
    m_ref[idx] = jnp.broadcast_to(m_new, m_ref.shape[1:])
    return h_out


def _mlstm_gates(gt_ref, gb_ref, tri):
    a = gt_ref[...] + gb_ref[...]
    lane = lax.broadcasted_iota(jnp.int32, a.shape, 1)
    is_forget = ((lane // M_HEADS) % 2) == 1
    act = jnp.where(is_forget, jax.nn.log_sigmoid(a), a) * LOG2_E
    cum = jnp.dot(tri, act, preferred_element_type=F32, precision=lax.Precision.HIGHEST)
    return cum, act.T, cum.T


def _mlstm_kernel(qf_ref, ktf_ref, vf_ref, gf_ref,
                  qb_ref, ktb_ref, vb_ref, gb_ref,
                  ktc_ref, vc_ref, gc_ref, gbias_ref,
                  hf_ref, hb_ref, c_ref, n_ref, m_ref):
    i = pl.program_id(1)
    L = MLSTM_CHUNK
    rows = lax.broadcasted_iota(jnp.int32, (L, L), 0)
    cols = lax.broadcasted_iota(jnp.int32, (L, L), 1)
    lower = cols <= rows
    upper = cols >= rows
    ones = jnp.ones((L, LANES), BF16)

    def run(q_ref, kt_ref, v_ref, g_ref, h_ref, direction):
        causal = direction == 0
        mask = lower if causal else upper
        cum, act_t, cum_t = _mlstm_gates(g_ref, gbias_ref, mask.astype(F32))
        for hd in range(M_HEADS):
            ci = 2 * M_HEADS * direction + hd
            cf = ci + M_HEADS
            g_col = cum[:, cf:cf + 1]
            b_tot = g_col[L - 1:L, :] if causal else g_col[0:1, :]
            ks = slice(hd * M_DQK, (hd + 1) * M_DQK)
            vs = slice(hd * M_DV, (hd + 1) * M_DV)
            q = None if q_ref is None else q_ref[:, ks]
            h = _mlstm_unit(q, kt_ref[ks, :], v_ref[:, vs], g_col, cum_t[cf:cf + 1, :], act_t[ci:ci + 1, :],
                            b_tot, mask, ones, c_ref, n_ref, m_ref, direction * M_HEADS + hd)
            if h is not None:
                h_ref[:, vs] = h.astype(h_ref.dtype)

    @pl.when(i == 0)
    def _():
        c_ref[...] = jnp.zeros_like(c_ref)
        n_ref[...] = jnp.zeros_like(n_ref)
        m_ref[...] = jnp.zeros_like(m_ref)
        run(None, ktc_ref, vc_ref, gc_ref, None, 0)
        run(None, ktc_ref, vc_ref, gc_ref, None, 1)

    @pl.when(i > 0)
    def _():
        run(qf_ref, ktf_ref, vf_ref, gf_ref, hf_ref, 0)
        run(qb_ref, ktb_ref, vb_ref, gb_ref, hb_ref, 1)


def _mlstm(q, kt, proj, small, ktc, proj_c, small_c, gate_bias):
    b, t, _ = q.shape
    L = MLSTM_CHUNK
    nc = t // L
    fwd = lambda bi, i: (bi, jnp.maximum(i - 1, 0), 0)
    bwd = lambda bi, i: (bi, jnp.minimum(nc - i, nc - 1), 0)
    fwd_t = lambda bi, i: (bi, 0, jnp.maximum(i - 1, 0))
    bwd_t = lambda bi, i: (bi, 0, jnp.minimum(nc - i, nc - 1))
    ctx = lambda bi, i: (bi, 0, 0)

    def specs(rm, tm_):
        return [pl.BlockSpec((None, L, M_QK_W), rm), pl.BlockSpec((None, M_QK_W, L), tm_),
                pl.BlockSpec((None, L, M_V_W), rm), pl.BlockSpec((None, L, LANES), rm)]

    in_specs = (specs(fwd, fwd_t) + specs(bwd, bwd_t)
                + [pl.BlockSpec((None, M_QK_W, L), ctx), pl.BlockSpec((None, L, M_V_W), ctx),
                   pl.BlockSpec((None, L, LANES), ctx), pl.BlockSpec((1, LANES), lambda bi, i: (0, 0))])
    return pl.pallas_call(
        _mlstm_kernel,
        grid=(b, nc + 1),
        in_specs=in_specs,
        out_specs=[pl.BlockSpec((None, L, M_V_W), fwd), pl.BlockSpec((None, L, M_V_W), bwd)],
        out_shape=[jax.ShapeDtypeStruct((b, t, M_V_W), BF16)] * 2,
        scratch_shapes=[pltpu.VMEM((2 * M_HEADS, M_DQK, M_DV), F32),
                        pltpu.VMEM((2 * M_HEADS, M_DQK, LANES), F32),
                        pltpu.VMEM((2 * M_HEADS, 8, LANES), F32)],
        compiler_params=_cparams(("parallel", "arbitrary")),
        name="mlstm",
    )(q, kt, proj, small, q, kt, proj, small, ktc, proj_c, small_c, gate_bias)


def _rope_partner_index():
    lane = np.arange(LANES)
    return np.where(lane % (2 * ROPE_FREQS) < ROPE_FREQS, lane + ROPE_FREQS, lane - ROPE_FREQS)


def _mla_prep_kernel(ckv_x_ref, sm_x_ref, ckv_c_ref, sm_c_ref, qa_ref,
                     wuk_ref, wuv_ref, kvg_ref, kgn_ref, kgr_ref, qgw_ref, qgs_ref, cos_ref, sin_ref,
                     pair_ref, sel_ref, half_ref, swap_ref,
                     k_ref, v_ref, q_ref):
    i = pl.program_id(1)
    rows = ckv_x_ref.shape[0]
    lane_id = lax.broadcasted_iota(jnp.int32, (rows, LANES), 1)
    low_half = lane_id < A_ROPE
    low_half_of_pair = (lane_id % (2 * ROPE_FREQS)) < ROPE_FREQS
    eps_dim = A_QK * EPS

    def keys_values(ckv_ref, sm_ref, rotate):
        ckv = ckv_ref[...].astype(F32)
        cn = (ckv * lax.rsqrt(jnp.mean(ckv * ckv, axis=-1, keepdims=True) + EPS) * kvg_ref[...]).astype(BF16)
        v_ref[...] = jnp.dot(cn, wuv_ref[...], preferred_element_type=F32).astype(v_ref.dtype)
        kn = jnp.dot(cn, wuk_ref[...], preferred_element_type=F32)
        sq = (kn * kn).astype(BF16)
        pair = pair_ref[...]
        ss = jnp.concatenate([jnp.dot(sq[:, g * 2 * A_NOPE:(g + 1) * 2 * A_NOPE], pair, preferred_element_type=F32)
                              for g in range(A_HEADS // 2)], axis=-1)
        kr2 = sm_ref[:, S_KR:S_KR + LANES]
        kr_ss = jnp.dot((kr2 * kr2).astype(BF16), half_ref[...], preferred_element_type=F32)
        r = lax.rsqrt(ss + jnp.concatenate([kr_ss + eps_dim] * (A_HEADS // 2), axis=-1))
        kns = kn * r * kgn_ref[...]
        krg = kr2 * kgr_ref[...]
        if rotate:
            partner = jnp.where(low_half_of_pair, pltpu.roll(krg, LANES - ROPE_FREQS, 1), pltpu.roll(krg, ROPE_FREQS, 1))
            krg = krg * cos_ref[...] + partner * sin_ref[...]
        for hd in range(A_HEADS):
            hs = slice(hd * A_NOPE, (hd + 1) * A_NOPE)
            kn_h = kns[:, hs]
            kr_h = krg * r[:, hs]
            if hd % 2 == 0:
                k_ref[hd, :, :LANES] = kn_h.astype(k_ref.dtype)
                k_ref[hd, :, LANES:] = jnp.where(low_half, kr_h, 0.0).astype(k_ref.dtype)
            else:
                k_ref[hd, :, :LANES] = jnp.where(low_half, 0.0, kn_h).astype(k_ref.dtype)
                k_ref[hd, :, LANES:] = jnp.where(low_half, kn_h, kr_h).astype(k_ref.dtype)

    @pl.when(i == 0)
    def _():
        keys_values(ckv_c_ref, sm_c_ref, False)

    @pl.when(i > 0)
    def _():
        keys_values(ckv_x_ref, sm_x_ref, True)
        cos_e, sin_e = cos_ref[...], sin_ref[...]
        cos_o, sin_o = jnp.where(low_half, 1.0, cos_e), jnp.where(low_half, 0.0, sin_e)
        swap = swap_ref[...]
        for hd in range(A_HEADS):
            odd = hd % 2
            lo = (hd // 2) * 3 * LANES + odd * LANES
            win = qa_ref[:, lo:lo + A_QKX]
            wf = win.astype(F32)
            ss = jnp.dot((wf * wf).astype(BF16), sel_ref[odd], preferred_element_type=F32)
            r = lax.rsqrt(ss + eps_dim)
            x1 = wf[:, :LANES] * qgw_ref[odd:odd + 1, :LANES]
            x2 = wf[:, LANES:] * qgw_ref[odd:odd + 1, LANES:]
            p2 = jnp.dot(win[:, LANES:], swap, preferred_element_type=F32) * qgs_ref[odd:odd + 1, :]
            rot2 = x2 * (cos_o if odd else cos_e) + p2 * (sin_o if odd else sin_e)
            q_ref[hd, :, :LANES] = (x1 * r).astype(q_ref.dtype)
            q_ref[hd, :, LANES:] = (rot2 * r).astype(q_ref.dtype)


def _mla_prep(proj, small, proj_c, small_c, w_uk, w_uv, kv_norm_g, k_norm_g, q_norm_g, cos_t, sin_t):
    b, t, _ = proj.shape
    tc = proj_c.shape[1]
    tr = tc
    nx = t // tr
    xrow = lambda bi, i: (bi, jnp.maximum(i - 1, 0), 0)
    const = lambda *shape: pl.BlockSpec(shape, lambda bi, i: (0,) * len(shape))

    halves = lambda a: a.reshape(a.shape[:-1] + (A_HEADS // 2, 2, 2, A_NOPE // 2))
    swap_odd = lambda a: jnp.concatenate([halves(a)[..., 0:1, :, :], halves(a)[..., 1:2, ::-1, :]],
                                         axis=-3).reshape(a.shape)
    w_uk_l = swap_odd(w_uk)
    gk_n, gk_r = k_norm_g[:A_NOPE] * A_QK ** 0.5, k_norm_g[A_NOPE:] * A_QK ** 0.5
    gq_n, gq_r = q_norm_g[:A_NOPE] * LOG2_E, q_norm_g[A_NOPE:] * LOG2_E
    kgn = swap_odd(jnp.tile(gk_n, A_HEADS)).reshape(1, A_HEADS * A_NOPE)
    kgr = jnp.tile(gk_r, 2).reshape(1, LANES)
    zeros = jnp.zeros((A_ROPE,), F32)
    qgw = jnp.stack([jnp.concatenate([gq_n, gq_r, zeros]), jnp.concatenate([zeros, gq_n, gq_r])])
    partner = _rope_partner_index()
    qgs = qgw[:, LANES:][:, partner]

    lane = np.arange(A_QKX)
    pair = (lane[:, None] // A_NOPE == lane[None, :] // A_NOPE).astype(np.float32)
    sel = np.stack([np.broadcast_to((lane < A_QK)[:, None], (A_QKX, LANES)),
                    np.broadcast_to((lane >= A_QKX - A_QK)[:, None], (A_QKX, LANES))]).astype(np.float32)
    half = np.broadcast_to((np.arange(LANES) < A_ROPE)[:, None], (LANES, A_QKX)).astype(np.float32)
    swap = np.zeros((LANES, LANES), np.float32)
    swap[partner, np.arange(LANES)] = 1.0
    as_bf16 = lambda a: jnp.asarray(a, BF16)

    return pl.pallas_call(
        _mla_prep_kernel,
        grid=(b, nx + 1),
        in_specs=[pl.BlockSpec((None, tr, KV_RANK), lambda bi, i: (bi, jnp.maximum(i - 1, 0), P_CKV // KV_RANK)),
                  pl.BlockSpec((None, tr, S_COLS), xrow),
                  pl.BlockSpec((None, tr, KV_RANK), lambda bi, i: (bi, 0, C_CKV // KV_RANK)),
                  pl.BlockSpec((None, tr, S_COLS), lambda bi, i: (bi, 0, 0)),
                  pl.BlockSpec((None, tr, A_Q_W), lambda bi, i: (bi, jnp.maximum(i - 1, 0), P_QA // A_Q_W)),
                  const(KV_RANK, A_HEADS * A_NOPE), const(KV_RANK, A_V_W), const(1, KV_RANK),
                  const(1, A_HEADS * A_NOPE), const(1, LANES), const(2, A_QKX), const(2, LANES),
                  pl.BlockSpec((tr, LANES), lambda bi, i: (jnp.maximum(i - 1, 0), 0)),
                  pl.BlockSpec((tr, LANES), lambda bi, i: (jnp.maximum(i - 1, 0), 0)),
                  const(A_QKX, A_QKX), const(2, A_QKX, LANES), const(LANES, A_QKX), const(LANES, LANES)],
        out_specs=[pl.BlockSpec((None, A_HEADS, tr, A_QKX), lambda bi, i: (bi, 0, i, 0)),
                   pl.BlockSpec((None, tr, A_V_W), lambda bi, i: (bi, i, 0)),
                   pl.BlockSpec((None, A_HEADS, tr, A_QKX), lambda bi, i: (bi, 0, jnp.maximum(i - 1, 0), 0))],
        out_shape=[jax.ShapeDtypeStruct((b, A_HEADS, tc + t, A_QKX), BF16),
                   jax.ShapeDtypeStruct((b, tc + t, A_V_W), BF16),
                   jax.ShapeDtypeStruct((b, A_HEADS, t, A_QKX), BF16)],
        compiler_params=_cparams(("parallel", "arbitrary")),
        name="mla_prep",
    )(proj, small, proj_c, small_c, proj, w_uk_l.astype(BF16), w_uv.astype(BF16), kv_norm_g.reshape(1, KV_RANK),
      kgn, kgr, qgw, qgs, cos_t, sin_t, as_bf16(pair), as_bf16(sel), as_bf16(half), as_bf16(swap))


def _attn_kernel(q_ref, k_ref, v_ref, onecol_ref, o_ref, *, key_blocks):
    q = q_ref[...]
    m = acc = None
    for lo, hi in key_blocks:
        s = lax.dot_general(q, k_ref[lo:hi, :], (((1,), (1,)), ((), ())), preferred_element_type=F32)
        m_blk = jnp.max(s, axis=-1, keepdims=True)
        m_new = m_blk if m is None else jnp.maximum(m, m_blk)
        p = jnp.exp2(s - m_new).astype(BF16)
        v_ext = jnp.concatenate([v_ref[lo:hi, :], onecol_ref[:hi - lo, :]], axis=1)
        pv = jnp.dot(p, v_ext, preferred_element_type=F32)
        acc = pv if acc is None else acc * jnp.exp2(m - m_new) + pv
        m = m_new
    o_ref[...] = (acc[:, :A_DV] / acc[:, A_DV:A_DV + 1]).astype(o_ref.dtype)


ATTN_KEY_BLOCK = 256


def _attention(q, k, v, n_ctx):
    b, h, s, _ = q.shape
    tk = k.shape[2]
    tq = min(1024, s)
    key_blocks = [(lo, lo + ATTN_KEY_BLOCK) for lo in range(n_ctx, tk, ATTN_KEY_BLOCK)] + [(0, n_ctx)]
    rows = max(hi - lo for lo, hi in key_blocks)
    onecol = np.zeros((rows, A_VX - A_DV), np.float32)
    onecol[:, 0] = 1.0
    return pl.pallas_call(
        functools.partial(_attn_kernel, key_blocks=tuple(key_blocks)),
        grid=(b, h, s // tq),
        in_specs=[pl.BlockSpec((None, None, tq, A_QKX), lambda bi, hi, i: (bi, hi, i, 0)),
                  pl.BlockSpec((None, None, tk, A_QKX), lambda bi, hi, i: (bi, hi, 0, 0)),
                  pl.BlockSpec((None, tk, A_DV), lambda bi, hi, i: (bi, 0, hi)),
                  pl.BlockSpec((rows, A_VX - A_DV), lambda bi, hi, i: (0, 0))],
        out_specs=pl.BlockSpec((None, tq, A_DV), lambda bi, hi, i: (bi, i, hi)),
        out_shape=jax.ShapeDtypeStruct((b, s, A_V_W), BF16),
        compiler_params=_cparams(("parallel", "parallel", "parallel")),
        name="attention",
    )(q, k, v, jnp.asarray(onecol, BF16))


def _merge_kernel(hf_ref, hb_ref, om_ref, zm_ref, oa_ref, za_ref, gm_ref, ga_ref, mhg_ref,
                  wm_ref, wa_ref, o_ref, hm_ref):
    for hd in range(M_HEADS):
        vs = slice(hd * M_DV, (hd + 1) * M_DV)
        h = hf_ref[:, vs].astype(F32) + hb_ref[:, vs].astype(F32)
        hn = h * lax.rsqrt(jnp.mean(h * h, axis=-1, keepdims=True) + EPS) * mhg_ref[:, vs]
        gated = hn * _sigmoid(om_ref[:, vs].astype(F32)) * _silu(zm_ref[:, vs].astype(F32))
        hm_ref[:, vs] = gated.astype(BF16)
    p_m = jnp.dot(hm_ref[...], wm_ref[...], preferred_element_type=F32)
    oa = (oa_ref[...].astype(F32) * _silu(za_ref[...].astype(F32))).astype(BF16)
    p_a = jnp.dot(oa, wa_ref[...], preferred_element_type=F32)
    y = _sigmoid(gm_ref[...].astype(F32)) * p_m + _sigmoid(ga_ref[...].astype(F32)) * p_a
    o_ref[...] = y.astype(o_ref.dtype)


def _merge(hf, hb, proj, oa, mh_norm_g, w_proj_m, w_proj_a):
    b, t, _ = hf.shape
    tm = 256
    row = lambda bi, i: (bi, i, 0)
    col = lambda c: (lambda bi, i: (bi, i, c))
    wspec = pl.BlockSpec((D_MODEL, D_MODEL), lambda bi, i: (0, 0))
    act = lambda im: pl.BlockSpec((None, tm, D_MODEL), im)
    return pl.pallas_call(
        _merge_kernel,
        grid=(b, t // tm),
        in_specs=[act(row), act(row),
                  act(col(P_OM // D_MODEL)), act(col(P_ZM // D_MODEL)),
                  act(row), act(col(P_ZA // D_MODEL)),
                  act(col(P_GM // D_MODEL)), act(col(P_GM // D_MODEL + 1)),
                  pl.BlockSpec((1, M_V_W), lambda bi, i: (0, 0)),
                  wspec, wspec],
        out_specs=act(row),
        out_shape=jax.ShapeDtypeStruct((b, t, D_MODEL), BF16),
        scratch_shapes=[pltpu.VMEM((tm, M_V_W), BF16)],
        compiler_params=_cparams(("parallel", "parallel")),
        name="merge",
    )(hf, hb, proj, proj, oa, proj, proj, proj, mh_norm_g.reshape(1, M_V_W), w_proj_m, w_proj_a)


def _out_kernel(y_ref, x_ref, gate_ref, w_ref, o_ref):
    o_ref[...] = x_ref[...] + gate_ref[...] * jnp.dot(y_ref[...], w_ref[...], preferred_element_type=F32)


def _out(y, x, gate, w_out):
    b, t, _ = x.shape
    tm = min(512, t)
    row = lambda bi, i: (bi, i, 0)
    return pl.pallas_call(
        _out_kernel,
        grid=(b, t // tm),
        in_specs=[pl.BlockSpec((None, tm, D_MODEL), row),
                  pl.BlockSpec((None, tm, D_MODEL), row),
                  pl.BlockSpec((None, 1, D_MODEL), lambda bi, i: (bi, 0, 0)),
                  pl.BlockSpec((D_MODEL, D_MODEL), lambda bi, i: (0, 0))],
        out_specs=pl.BlockSpec((None, tm, D_MODEL), row),
        out_shape=jax.ShapeDtypeStruct((b, t, D_MODEL), F32),
        compiler_params=_cparams(("parallel", "parallel")),
        name="out",
    )(y, x, gate, w_out)


def _layout_w_in(w):
    km, vm, ckv, kr = w[:, _O_KM:_O_VM], w[:, _O_VM:_O_GT], w[:, _O_CKV:_O_KR], w[:, _O_KR:_O_QM]
    main = jnp.concatenate([vm, w[:, _O_OM:_O_ZM], w[:, _O_ZM:_O_QA], w[:, _O_ZA:_O_GM], w[:, _O_GM:_O_END],
                            w[:, _O_QA:_O_ZA], km, w[:, _O_QM:_O_OM], ckv], axis=1).astype(BF16)
    ctx = jnp.concatenate([vm, km, ckv], axis=1).astype(BF16)
    small = jnp.concatenate([w[:, _O_GT:_O_CKV], jnp.zeros((D_MODEL, S_KR - M_GATE_W), w.dtype), kr, kr],
                            axis=1).astype(BF16)
    return main, ctx, small


def _rope_tables(seq):
    pos = np.arange(seq)
    lane = np.arange(LANES) % A_ROPE
    axis = lane // (2 * ROPE_FREQS)
    half = (lane % (2 * ROPE_FREQS)) // ROPE_FREQS
    freqs = ROPE_THETA ** (-np.arange(ROPE_FREQS, dtype=np.float64) / ROPE_FREQS)
    coord = np.where((axis == 0)[None, :], (pos // GRID_W)[:, None], (pos % GRID_W)[:, None]).astype(np.float64)
    ang = coord * freqs[lane % ROPE_FREQS][None, :]
    sign = np.where(half == 0, -1.0, 1.0)[None, :]
    return jnp.asarray(np.cos(ang), F32), jnp.asarray(np.sin(ang) * sign, F32)


def _layer(x, c, ctx, c_ctx, ada_w, ada_b, norm_g, w_in, conv_w, conv_b, gate_b, mh_norm_g, q_norm_g, k_norm_g,
           kv_norm_g, w_uk, w_uv, w_proj_m, w_proj_a, w_out):
    b, t, _ = x.shape
    tc = ctx.shape[1]
    assert tc == MLSTM_CHUNK and t % MLSTM_CHUNK == 0 and t % GRID_W == 0

    c8 = jnp.zeros((8, D_MODEL), F32).at[:b].set(c).at[b].set(c_ctx)
    mod = _adaln(c8, ada_w, ada_b)
    shift, scale, gate = mod[:, :D_MODEL], mod[:, D_MODEL:2 * D_MODEL], mod[:, 2 * D_MODEL:]
    per_b = lambda a: a[:b].reshape(b, 1, D_MODEL)
    per_c = lambda a: jnp.broadcast_to(a[b].reshape(1, 1, D_MODEL), (b, 1, D_MODEL))

    w_main, w_ctx, w_small = _layout_w_in(w_in)
    proj, small = _proj(x, per_b(scale), per_b(shift), norm_g, w_main, w_small, P_COLS, 2560)
    proj_c, small_c = _proj(ctx, per_c(scale), per_c(shift), norm_g, w_ctx, w_small, C_COLS, C_COLS // 2)

    cw_q, cw_k = conv_w[:, :M_QK_W], conv_w[:, M_QK_W:]
    cb_q, cb_k = conv_b[:M_QK_W], conv_b[M_QK_W:]
    q_m = _conv(proj, P_QM, cw_q, cb_q, M_DQK ** -0.5, False)
    kt_m = _conv(proj, P_KM, cw_k, cb_k, 1.0, True)
    kt_c = _conv(proj_c, C_KM, cw_k, cb_k, 1.0, True)
    gate_bias = jnp.zeros((1, LANES), F32).at[0, :M_GATE_W].set(gate_b)
    hf, hb = _mlstm(q_m, kt_m, proj, small, kt_c, proj_c, small_c, gate_bias)

    cos_t, sin_t = _rope_tables(t)
    k_a, v_a, q_a = _mla_prep(proj, small, proj_c, small_c, w_uk, w_uv, kv_norm_g, k_norm_g, q_norm_g, cos_t, sin_t)
    o_a = _attention(q_a, k_a, v_a, tc)

    y = _merge(hf, hb, proj, o_a, mh_norm_g, w_proj_m.astype(BF16), w_proj_a.astype(BF16))
    return _out(y, x, per_b(gate), w_out.astype(BF16))


def kernel(x, c, ctx, c_ctx, ada_w, ada_b, norm_g, w_in, conv_w, conv_b, gate_b, mh_norm_g, q_norm_g, k_norm_g,
           kv_norm_g, w_uk, w_uv, w_proj_m, w_proj_a, w_out):
    assert ada_w.shape[0] == 1, "single-layer block"
    return _layer(x, c, ctx, c_ctx, ada_w[0], ada_b[0], norm_g[0], w_in[0], conv_w[0], conv_b[0], gate_b[0],
                  mh_norm_g[0], q_norm_g[0], k_norm_g[0], kv_norm_g[0], w_uk[0], w_uv[0], w_proj_m[0],
                  w_proj_a[0], w_out[0])
```

```python
import functools

import numpy as np
import jax
import jax.numpy as jnp
from jax import lax
from jax.experimental import pallas as pl
from jax.experimental.pallas import tpu as pltpu

F32 = jnp.float32
BF16 = jnp.bfloat16

D_MODEL = 2048
GRID_W = 64
EPS = 1e-6

M_HEADS = 8
M_DQK = 128
M_DV = 256
M_CONV = 5
A_HEADS = 16
A_NOPE = 128
A_ROPE = 64
A_QK = A_NOPE + A_ROPE
A_DV = 128
A_VX = 256
A_QKX = 256
KV_RANK = 512
ROPE_FREQS = A_ROPE // 4
ROPE_THETA = 10000.0

M_QK_W = M_HEADS * M_DQK
M_V_W = M_HEADS * M_DV
M_GATE_W = 4 * M_HEADS
A_Q_W = A_HEADS * A_QK
A_V_W = A_HEADS * A_DV

_O_KM = 0
_O_VM = _O_KM + M_QK_W
_O_GT = _O_VM + M_V_W
_O_CKV = _O_GT + M_GATE_W
_O_KR = _O_CKV + KV_RANK
_O_QM = _O_KR + A_ROPE
_O_OM = _O_QM + M_QK_W
_O_ZM = _O_OM + M_V_W
_O_QA = _O_ZM + M_V_W
_O_ZA = _O_QA + A_Q_W
_O_GM = _O_ZA + A_V_W
_O_END = _O_GM + 2 * D_MODEL

P_VM = 0
P_OM = P_VM + M_V_W
P_ZM = P_OM + M_V_W
P_ZA = P_ZM + M_V_W
P_GM = P_ZA + A_V_W
P_QA = P_GM + 2 * D_MODEL
P_KM = P_QA + A_Q_W
P_QM = P_KM + M_QK_W
P_CKV = P_QM + M_QK_W
P_COLS = P_CKV + KV_RANK
C_VM = 0
C_KM = C_VM + M_V_W
C_CKV = C_KM + M_QK_W
C_COLS = C_CKV + KV_RANK
S_GT = 0
S_KR = 128
S_COLS = 256

LANES = 128
MLSTM_CHUNK = 256
MLSTM_CHUNKS_PER_STEP = 2
NEG_BIG = -1e30
LOG2_E = 1.4426950408889634
VMEM_LIMIT = 60 * 1024 * 1024


def _cparams(sem):
    return pltpu.CompilerParams(dimension_semantics=sem, vmem_limit_bytes=VMEM_LIMIT)


def _sigmoid(a):
    return 0.5 * jnp.tanh(0.5 * a) + 0.5


def _silu(a):
    half = 0.5 * a
    return half * jnp.tanh(half) + half


def _adaln_kernel(c_ref, w_ref, b_ref, o_ref):
    s = _silu(c_ref[...])
    o_ref[...] = jnp.dot(s.astype(BF16), w_ref[...].astype(BF16), preferred_element_type=F32) + b_ref[...]


def _adaln(c8, ada_w, ada_b):
    n = ada_w.shape[1]
    tn = 1024
    return pl.pallas_call(
        _adaln_kernel,
        grid=(n // tn,),
        in_specs=[pl.BlockSpec((8, D_MODEL), lambda j: (0, 0)),
                  pl.BlockSpec((D_MODEL, tn), lambda j: (0, j)),
                  pl.BlockSpec((1, tn), lambda j: (0, j))],
        out_specs=pl.BlockSpec((8, tn), lambda j: (0, j)),
        out_shape=jax.ShapeDtypeStruct((8, n), F32),
        compiler_params=_cparams(("parallel",)),
        name="adaln",
    )(c8, ada_w, ada_b.reshape(1, n))


def _proj_kernel(x_ref, sc_ref, sh_ref, g_ref, w_ref, ws_ref, o_ref, os_ref, h_ref):
    @pl.when(pl.program_id(2) == 0)
    def _():
        x = x_ref[...]
        ms = jnp.mean(x * x, axis=-1, keepdims=True)
        y = x * lax.rsqrt(ms + EPS) * g_ref[...]
        h = (y * (1.0 + sc_ref[...]) + sh_ref[...]).astype(BF16)
        h_ref[...] = h
        os_ref[...] = jnp.dot(h, ws_ref[...], preferred_element_type=F32)

    o_ref[...] = jnp.dot(h_ref[...], w_ref[...], preferred_element_type=F32).astype(o_ref.dtype)


def _proj(x, scale, shift, norm_g, w_main, w_small, n_cols, tn):
    b, t, _ = x.shape
    tm = min(1024, t)
    return pl.pallas_call(
        _proj_kernel,
        grid=(b, t // tm, n_cols // tn),
        in_specs=[pl.BlockSpec((None, tm, D_MODEL), lambda bi, i, j: (bi, i, 0)),
                  pl.BlockSpec((None, 1, D_MODEL), lambda bi, i, j: (bi, 0, 0)),
                  pl.BlockSpec((None, 1, D_MODEL), lambda bi, i, j: (bi, 0, 0)),
                  pl.BlockSpec((1, D_MODEL), lambda bi, i, j: (0, 0)),
                  pl.BlockSpec((D_MODEL, tn), lambda bi, i, j: (0, j)),
                  pl.BlockSpec((D_MODEL, S_COLS), lambda bi, i, j: (0, 0))],
        out_specs=[pl.BlockSpec((None, tm, tn), lambda bi, i, j: (bi, i, j)),
                   pl.BlockSpec((None, tm, S_COLS), lambda bi, i, j: (bi, i, 0))],
        out_shape=[jax.ShapeDtypeStruct((b, t, n_cols), BF16),
                   jax.ShapeDtypeStruct((b, t, S_COLS), F32)],
        scratch_shapes=[pltpu.VMEM((tm, D_MODEL), BF16)],
        compiler_params=_cparams(("parallel", "parallel", "arbitrary")),
        name="proj",
    )(x, scale, shift, norm_g.reshape(1, D_MODEL), w_main, w_small)


CONV_HALO = 16


def _conv_kernel(prev_ref, cur_ref, next_ref, w_ref, b_ref, o_ref, buf_ref, *, tt, nt, out_scale, transpose):
    i = pl.program_id(1)
    buf_ref[CONV_HALO:CONV_HALO + tt, :] = cur_ref[...].astype(F32)
    buf_ref[0:CONV_HALO, :] = prev_ref[...].astype(F32) * (i > 0).astype(F32)
    buf_ref[CONV_HALO + tt:2 * CONV_HALO + tt, :] = next_ref[...].astype(F32) * (i < nt - 1).astype(F32)
    acc = jnp.broadcast_to(b_ref[...], (tt, cur_ref.shape[-1]))
    for k in range(M_CONV):
        lo = CONV_HALO - M_CONV // 2 + k
        acc = acc + w_ref[k:k + 1, :] * buf_ref[lo:lo + tt, :]
    y = _silu(acc)
    if out_scale != 1.0:
        y = y * out_scale
    o_ref[...] = (y.T if transpose else y).astype(o_ref.dtype)


def _conv(src, col_off, w, bias, out_scale, transpose):
    b, t, _ = src.shape
    cw = 512
    tt = min(512, t)
    nt = t // tt
    cb = col_off // cw
    hb = tt // CONV_HALO
    nhalo = t // CONV_HALO
    if transpose:
        out_shape = jax.ShapeDtypeStruct((b, M_QK_W, t), BF16)
        out_specs = pl.BlockSpec((None, cw, tt), lambda bi, i, c: (bi, c, i))
    else:
        out_shape = jax.ShapeDtypeStruct((b, t, M_QK_W), BF16)
        out_specs = pl.BlockSpec((None, tt, cw), lambda bi, i, c: (bi, i, c))
    return pl.pallas_call(
        functools.partial(_conv_kernel, tt=tt, nt=nt, out_scale=out_scale, transpose=transpose),
        grid=(b, nt, M_QK_W // cw),
        in_specs=[pl.BlockSpec((None, CONV_HALO, cw), lambda bi, i, c: (bi, jnp.maximum(i * hb - 1, 0), cb + c)),
                  pl.BlockSpec((None, tt, cw), lambda bi, i, c: (bi, i, cb + c)),
                  pl.BlockSpec((None, CONV_HALO, cw), lambda bi, i, c: (bi, jnp.minimum((i + 1) * hb, nhalo - 1), cb + c)),
                  pl.BlockSpec((M_CONV, cw), lambda bi, i, c: (0, c)),
                  pl.BlockSpec((1, cw), lambda bi, i, c: (0, c))],
        out_specs=out_specs,
        out_shape=out_shape,
        scratch_shapes=[pltpu.VMEM((tt + 2 * CONV_HALO, cw), F32)],
        compiler_params=_cparams(("parallel", "parallel", "parallel")),
        name="conv_t" if transpose else "conv",
    )(src, src, src, w, bias.reshape(1, M_QK_W))


def _mlstm_unit(q, kt, v, g_col, g_row, ig_row, b_tot, mask, ones, c_ref, n_ref, m_ref, idx):
    m_prev = m_ref[idx][0:1, 0:1]
    c_prev = c_ref[idx]
    n_prev = n_ref[idx]
    a_row = ig_row - g_row
    h_out = None
    if q is not None:
        am = jnp.where(mask, a_row, NEG_BIG)
        c_col = jnp.maximum(m_prev, jnp.max(am, axis=-1, keepdims=True))
        c_b = jnp.broadcast_to(c_col, mask.shape)
        s = (jnp.dot(q, kt, preferred_element_type=F32) * jnp.exp2(am - c_b)).astype(BF16)
        w_state = jnp.exp2(m_prev - c_b)
        num = (jnp.dot(s, v, preferred_element_type=F32)
               + w_state * jnp.dot(q, c_prev.astype(BF16), preferred_element_type=F32))
        den = (jnp.dot(s, ones, preferred_element_type=F32)
               + w_state[:, :LANES] * jnp.dot(q, n_prev.astype(BF16), preferred_element_type=F32))
        den = jnp.maximum(jnp.abs(den), jnp.exp2(-(g_col + c_col)))
        h_out = num / jnp.concatenate([den] * (num.shape[-1] // LANES), axis=-1)
    w_row = b_tot + a_row
    m_new = jnp.maximum(b_tot + m_prev, jnp.max(w_row, axis=-1, keepdims=True))
    decay = jnp.exp2(b_tot + m_prev - m_new)
    kw = kt * jnp.exp2(w_row - m_new).astype(BF16)
    c_ref[idx] = decay * c_prev + jnp.dot(kw, v, preferred_element_type=F32)
    n_ref[idx] = decay * n_prev + jnp.dot(kw, ones, preferred_element_type=F32)
    m_ref[idx] = jnp.broadcast_to(m_new, m_ref.shape[1:])
    return h_out


def _mlstm_gates(gt_ref, gb_ref, tri, rs):
    a = gt_ref[rs, :] + gb_ref[...]
    lane = lax.broadcasted_iota(jnp.int32, a.shape, 1)
    is_forget = ((lane // M_HEADS) % 2) == 1
    act = jnp.where(is_forget, jax.nn.log_sigmoid(a), a) * LOG2_E
    cum = jnp.dot(tri, act, preferred_element_type=F32, precision=lax.Precision.HIGHEST)
    return cum, act.T, cum.T


def _mlstm_kernel(qf_ref, ktf_ref, vf_ref, gf_ref,
                  qb_ref, ktb_ref, vb_ref, gb_ref,
                  ktc_ref, vc_ref, gc_ref, gbias_ref,
                  hf_ref, hb_ref, c_ref, n_ref, m_ref):
    i = pl.program_id(1)
    L = MLSTM_CHUNK
    rows = lax.broadcasted_iota(jnp.int32, (L, L), 0)
    cols = lax.broadcasted_iota(jnp.int32, (L, L), 1)
    lower = cols <= rows
    upper = cols >= rows
    ones = jnp.ones((L, LANES), BF16)

    def run(q_ref, kt_ref, v_ref, g_ref, h_ref, direction, sub):
        causal = direction == 0
        mask = lower if causal else upper
        rs = slice(sub * L, (sub + 1) * L)
        cum, act_t, cum_t = _mlstm_gates(g_ref, gbias_ref, mask.astype(F32), rs)
        for hd in range(M_HEADS):
            ci = 2 * M_HEADS * direction + hd
            cf = ci + M_HEADS
            g_col = cum[:, cf:cf + 1]
            b_tot = g_col[L - 1:L, :] if causal else g_col[0:1, :]
            ks = slice(hd * M_DQK, (hd + 1) * M_DQK)
            vs = slice(hd * M_DV, (hd + 1) * M_DV)
            q = None if q_ref is None else q_ref[rs, ks]
            h = _mlstm_unit(q, kt_ref[ks, rs], v_ref[rs, vs], g_col, cum_t[cf:cf + 1, :], act_t[ci:ci + 1, :],
                            b_tot, mask, ones, c_ref, n_ref, m_ref, direction * M_HEADS + hd)
            if h is not None:
                h_ref[rs, vs] = h.astype(h_ref.dtype)

    @pl.when(i == 0)
    def _():
        c_ref[...] = jnp.zeros_like(c_ref)
        n_ref[...] = jnp.zeros_like(n_ref)
        m_ref[...] = jnp.zeros_like(m_ref)
        run(None, ktc_ref, vc_ref, gc_ref, None, 0, 0)
        run(None, ktc_ref, vc_ref, gc_ref, None, 1, 0)

    @pl.when(i > 0)
    def _():
        for sub in range(MLSTM_CHUNKS_PER_STEP):
            run(qf_ref, ktf_ref, vf_ref, gf_ref, hf_ref, 0, sub)
            run(qb_ref, ktb_ref, vb_ref, gb_ref, hb_ref, 1, MLSTM_CHUNKS_PER_STEP - 1 - sub)


def _mlstm(q, kt, proj, small, ktc, proj_c, small_c, gate_bias):
    b, t, _ = q.shape
    L = MLSTM_CHUNK
    rows = MLSTM_CHUNKS_PER_STEP * L
    assert t % rows == 0
    nb = t // rows
    fwd = lambda bi, i: (bi, jnp.maximum(i - 1, 0), 0)
    bwd = lambda bi, i: (bi, jnp.minimum(nb - i, nb - 1), 0)
    fwd_t = lambda bi, i: (bi, 0, jnp.maximum(i - 1, 0))
    bwd_t = lambda bi, i: (bi, 0, jnp.minimum(nb - i, nb - 1))
    ctx = lambda bi, i: (bi, 0, 0)

    def specs(rm, tm_):
        return [pl.BlockSpec((None, rows, M_QK_W), rm), pl.BlockSpec((None, M_QK_W, rows), tm_),
                pl.BlockSpec((None, rows, M_V_W), rm), pl.BlockSpec((None, rows, LANES), rm)]

    in_specs = (specs(fwd, fwd_t) + specs(bwd, bwd_t)
                + [pl.BlockSpec((None, M_QK_W, L), ctx), pl.BlockSpec((None, L, M_V_W), ctx),
                   pl.BlockSpec((None, L, LANES), ctx), pl.BlockSpec((1, LANES), lambda bi, i: (0, 0))])
    return pl.pallas_call(
        _mlstm_kernel,
        grid=(b, nb + 1),
        in_specs=in_specs,
        out_specs=[pl.BlockSpec((None, rows, M_V_W), fwd), pl.BlockSpec((None, rows, M_V_W), bwd)],
        out_shape=[jax.ShapeDtypeStruct((b, t, M_V_W), BF16)] * 2,
        scratch_shapes=[pltpu.VMEM((2 * M_HEADS, M_DQK, M_DV), F32),
                        pltpu.VMEM((2 * M_HEADS, M_DQK, LANES), F32),
                        pltpu.VMEM((2 * M_HEADS, 8, LANES), F32)],
        compiler_params=_cparams(("parallel", "arbitrary")),
        name="mlstm",
    )(q, kt, proj, small, q, kt, proj, small, ktc, proj_c, small_c, gate_bias)


def _rope_partner_index():
    lane = np.arange(LANES)
    return np.where(lane % (2 * ROPE_FREQS) < ROPE_FREQS, lane + ROPE_FREQS, lane - ROPE_FREQS)


def _mla_prep_kernel(ckv_x_ref, sm_x_ref, ckv_c_ref, sm_c_ref, qa_ref,
                     wuk_ref, wuv_ref, kvg_ref, kgn_ref, kgr_ref, qgw_ref, qgs_ref, cos_ref, sin_ref,
                     pair_ref, sel_ref, half_ref, swap_ref,
                     k_ref, v_ref, q_ref):
    i = pl.program_id(1)
    rows = ckv_x_ref.shape[0]
    lane_id = lax.broadcasted_iota(jnp.int32, (rows, LANES), 1)
    low_half = lane_id < A_ROPE
    low_half_of_pair = (lane_id % (2 * ROPE_FREQS)) < ROPE_FREQS
    eps_dim = A_QK * EPS

    def keys_values(ckv_ref, sm_ref, rotate):
        ckv = ckv_ref[...].astype(F32)
        cn = (ckv * lax.rsqrt(jnp.mean(ckv * ckv, axis=-1, keepdims=True) + EPS) * kvg_ref[...]).astype(BF16)
        v_ref[...] = jnp.dot(cn, wuv_ref[...], preferred_element_type=F32).astype(v_ref.dtype)
        kn = jnp.dot(cn, wuk_ref[...], preferred_element_type=F32)
        sq = (kn * kn).astype(BF16)
        pair = pair_ref[...]
        ss = jnp.concatenate([jnp.dot(sq[:, g * 2 * A_NOPE:(g + 1) * 2 * A_NOPE], pair, preferred_element_type=F32)
                              for g in range(A_HEADS // 2)], axis=-1)
        kr2 = sm_ref[:, S_KR:S_KR + LANES]
        kr_ss = jnp.dot((kr2 * kr2).astype(BF16), half_ref[...], preferred_element_type=F32)
        r = lax.rsqrt(ss + jnp.concatenate([kr_ss + eps_dim] * (A_HEADS // 2), axis=-1))
        kns = kn * r * kgn_ref[...]
        krg = kr2 * kgr_ref[...]
        if rotate:
            partner = jnp.where(low_half_of_pair, pltpu.roll(krg, LANES - ROPE_FREQS, 1), pltpu.roll(krg, ROPE_FREQS, 1))
            krg = krg * cos_ref[...] + partner * sin_ref[...]
        for hd in range(A_HEADS):
            hs = slice(hd * A_NOPE, (hd + 1) * A_NOPE)
            kn_h = kns[:, hs]
            kr_h = krg * r[:, hs]
            if hd % 2 == 0:
                k_ref[hd, :, :LANES] = kn_h.astype(k_ref.dtype)
                k_ref[hd, :, LANES:] = jnp.where(low_half, kr_h, 0.0).astype(k_ref.dtype)
            else:
                k_ref[hd, :, :LANES] = jnp.where(low_half, 0.0, kn_h).astype(k_ref.dtype)
                k_ref[hd, :, LANES:] = jnp.where(low_half, kn_h, kr_h).astype(k_ref.dtype)

    @pl.when(i == 0)
    def _():
        keys_values(ckv_c_ref, sm_c_ref, False)

    @pl.when(i > 0)
    def _():
        keys_values(ckv_x_ref, sm_x_ref, True)
        cos_e, sin_e = cos_ref[...], sin_ref[...]
        cos_o, sin_o = jnp.where(low_half, 1.0, cos_e), jnp.where(low_half, 0.0, sin_e)
        swap = swap_ref[...]
        for hd in range(A_HEADS):
            odd = hd % 2
            lo = (hd // 2) * 3 * LANES + odd * LANES
            win = qa_ref[:, lo:lo + A_QKX]
            wf = win.astype(F32)
            ss = jnp.dot((wf * wf).astype(BF16), sel_ref[odd], preferred_element_type=F32)
            r = lax.rsqrt(ss + eps_dim)
            x1 = wf[:, :LANES] * qgw_ref[odd:odd + 1, :LANES]
            x2 = wf[:, LANES:] * qgw_ref[odd:odd + 1, LANES:]
            p2 = jnp.dot(win[:, LANES:], swap, preferred_element_type=F32) * qgs_ref[odd:odd + 1, :]
            rot2 = x2 * (cos_o if odd else cos_e) + p2 * (sin_o if odd else sin_e)
            q_ref[hd, :, :LANES] = (x1 * r).astype(q_ref.dtype)
            q_ref[hd, :, LANES:] = (rot2 * r).astype(q_ref.dtype)


def _mla_prep(proj, small, proj_c, small_c, w_uk, w_uv, kv_norm_g, k_norm_g, q_norm_g, cos_t, sin_t):
    b, t, _ = proj.shape
    tc = proj_c.shape[1]
    tr = tc
    nx = t // tr
    xrow = lambda bi, i: (bi, jnp.maximum(i - 1, 0), 0)
    const = lambda *shape: pl.BlockSpec(shape, lambda bi, i: (0,) * len(shape))

    halves = lambda a: a.reshape(a.shape[:-1] + (A_HEADS // 2, 2, 2, A_NOPE // 2))
    swap_odd = lambda a: jnp.concatenate([halves(a)[..., 0:1, :, :], halves(a)[..., 1:2, ::-1, :]],
                                         axis=-3).reshape(a.shape)
    w_uk_l = swap_odd(w_uk)
    gk_n, gk_r = k_norm_g[:A_NOPE] * A_QK ** 0.5, k_norm_g[A_NOPE:] * A_QK ** 0.5
    gq_n, gq_r = q_norm_g[:A_NOPE] * LOG2_E, q_norm_g[A_NOPE:] * LOG2_E
    kgn = swap_odd(jnp.tile(gk_n, A_HEADS)).reshape(1, A_HEADS * A_NOPE)
    kgr = jnp.tile(gk_r, 2).reshape(1, LANES)
    zeros = jnp.zeros((A_ROPE,), F32)
    qgw = jnp.stack([jnp.concatenate([gq_n, gq_r, zeros]), jnp.concatenate([zeros, gq_n, gq_r])])
    partner = _rope_partner_index()
    qgs = qgw[:, LANES:][:, partner]

    lane = np.arange(A_QKX)
    pair = (lane[:, None] // A_NOPE == lane[None, :] // A_NOPE).astype(np.float32)
    sel = np.stack([np.broadcast_to((lane < A_QK)[:, None], (A_QKX, LANES)),
                    np.broadcast_to((lane >= A_QKX - A_QK)[:, None], (A_QKX, LANES))]).astype(np.float32)
    half = np.broadcast_to((np.arange(LANES) < A_ROPE)[:, None], (LANES, A_QKX)).astype(np.float32)
    swap = np.zeros((LANES, LANES), np.float32)
    swap[partner, np.arange(LANES)] = 1.0
    as_bf16 = lambda a: jnp.asarray(a, BF16)

    return pl.pallas_call(
        _mla_prep_kernel,
        grid=(b, nx + 1),
        in_specs=[pl.BlockSpec((None, tr, KV_RANK), lambda bi, i: (bi, jnp.maximum(i - 1, 0), P_CKV // KV_RANK)),
                  pl.BlockSpec((None, tr, S_COLS), xrow),
                  pl.BlockSpec((None, tr, KV_RANK), lambda bi, i: (bi, 0, C_CKV // KV_RANK)),
                  pl.BlockSpec((None, tr, S_COLS), lambda bi, i: (bi, 0, 0)),
                  pl.BlockSpec((None, tr, A_Q_W), lambda bi, i: (bi, jnp.maximum(i - 1, 0), P_QA // A_Q_W)),
                  const(KV_RANK, A_HEADS * A_NOPE), const(KV_RANK, A_V_W), const(1, KV_RANK),
                  const(1, A_HEADS * A_NOPE), const(1, LANES), const(2, A_QKX), const(2, LANES),
                  pl.BlockSpec((tr, LANES), lambda bi, i: (jnp.maximum(i - 1, 0), 0)),
                  pl.BlockSpec((tr, LANES), lambda bi, i: (jnp.maximum(i - 1, 0), 0)),
                  const(A_QKX, A_QKX), const(2, A_QKX, LANES), const(LANES, A_QKX), const(LANES, LANES)],
        out_specs=[pl.BlockSpec((None, A_HEADS, tr, A_QKX), lambda bi, i: (bi, 0, i, 0)),
                   pl.BlockSpec((None, tr, A_V_W), lambda bi, i: (bi, i, 0)),
                   pl.BlockSpec((None, A_HEADS, tr, A_QKX), lambda bi, i: (bi, 0, jnp.maximum(i - 1, 0), 0))],
        out_shape=[jax.ShapeDtypeStruct((b, A_HEADS, tc + t, A_QKX), BF16),
                   jax.ShapeDtypeStruct((b, tc + t, A_V_W), BF16),
                   jax.ShapeDtypeStruct((b, A_HEADS, t, A_QKX), BF16)],
        compiler_params=_cparams(("parallel", "arbitrary")),
        name="mla_prep",
    )(proj, small, proj_c, small_c, proj, w_uk_l.astype(BF16), w_uv.astype(BF16), kv_norm_g.reshape(1, KV_RANK),
      kgn, kgr, qgw, qgs, cos_t, sin_t, as_bf16(pair), as_bf16(sel), as_bf16(half), as_bf16(swap))


def _attn_kernel(q_ref, k_ref, v_ref, onecol_ref, o_ref, *, key_blocks):
    q = q_ref[...]
    m = acc = None
    for lo, hi in key_blocks:
        s = lax.dot_general(q, k_ref[lo:hi, :], (((1,), (1,)), ((), ())), preferred_element_type=F32)
        m_blk = jnp.max(s, axis=-1, keepdims=True)
        m_new = m_blk if m is None else jnp.maximum(m, m_blk)
        p = jnp.exp2(s - m_new).astype(BF16)
        v_ext = jnp.concatenate([v_ref[lo:hi, :], onecol_ref[:hi - lo, :]], axis=1)
        pv = jnp.dot(p, v_ext, preferred_element_type=F32)
        acc = pv if acc is None else acc * jnp.exp2(m - m_new) + pv
        m = m_new
    o_ref[...] = (acc[:, :A_DV] / acc[:, A_DV:A_DV + 1]).astype(o_ref.dtype)


ATTN_KEY_BLOCK = 256


def _attention(q, k, v, n_ctx):
    b, h, s, _ = q.shape
    tk = k.shape[2]
    tq = min(1024, s)
    key_blocks = [(lo, lo + ATTN_KEY_BLOCK) for lo in range(n_ctx, tk, ATTN_KEY_BLOCK)] + [(0, n_ctx)]
    rows = max(hi - lo for lo, hi in key_blocks)
    onecol = np.zeros((rows, A_VX - A_DV), np.float32)
    onecol[:, 0] = 1.0
    return pl.pallas_call(
        functools.partial(_attn_kernel, key_blocks=tuple(key_blocks)),
        grid=(b, h, s // tq),
        in_specs=[pl.BlockSpec((None, None, tq, A_QKX), lambda bi, hi, i: (bi, hi, i, 0)),
                  pl.BlockSpec((None, None, tk, A_QKX), lambda bi, hi, i: (bi, hi, 0, 0)),
                  pl.BlockSpec((None, tk, A_DV), lambda bi, hi, i: (bi, 0, hi)),
                  pl.BlockSpec((rows, A_VX - A_DV), lambda bi, hi, i: (0, 0))],
        out_specs=pl.BlockSpec((None, tq, A_DV), lambda bi, hi, i: (bi, i, hi)),
        out_shape=jax.ShapeDtypeStruct((b, s, A_V_W), BF16),
        compiler_params=_cparams(("parallel", "parallel", "parallel")),
        name="attention",
    )(q, k, v, jnp.asarray(onecol, BF16))


def _merge_kernel(hf_ref, hb_ref, om_ref, zm_ref, oa_ref, za_ref, gm_ref, ga_ref, mhg_ref,
                  wm_ref, wa_ref, o_ref, hm_ref):
    for hd in range(M_HEADS):
        vs = slice(hd * M_DV, (hd + 1) * M_DV)
        h = hf_ref[:, vs].astype(F32) + hb_ref[:, vs].astype(F32)
        hn = h * lax.rsqrt(jnp.mean(h * h, axis=-1, keepdims=True) + EPS) * mhg_ref[:, vs]
        gated = hn * _sigmoid(om_ref[:, vs].astype(F32)) * _silu(zm_ref[:, vs].astype(F32))
        hm_ref[:, vs] = gated.astype(BF16)
    p_m = jnp.dot(hm_ref[...], wm_ref[...], preferred_element_type=F32)
    oa = (oa_ref[...].astype(F32) * _silu(za_ref[...].astype(F32))).astype(BF16)
    p_a = jnp.dot(oa, wa_ref[...], preferred_element_type=F32)
    y = _sigmoid(gm_ref[...].astype(F32)) * p_m + _sigmoid(ga_ref[...].astype(F32)) * p_a
    o_ref[...] = y.astype(o_ref.dtype)


def _merge(hf, hb, proj, oa, mh_norm_g, w_proj_m, w_proj_a):
    b, t, _ = hf.shape
    tm = 256
    row = lambda bi, i: (bi, i, 0)
    col = lambda c: (lambda bi, i: (bi, i, c))
    wspec = pl.BlockSpec((D_MODEL, D_MODEL), lambda bi, i: (0, 0))
    act = lambda im: pl.BlockSpec((None, tm, D_MODEL), im)
    return pl.pallas_call(
        _merge_kernel,
        grid=(b, t // tm),
        in_specs=[act(row), act(row),
                  act(col(P_OM // D_MODEL)), act(col(P_ZM // D_MODEL)),
                  act(row), act(col(P_ZA // D_MODEL)),
                  act(col(P_GM // D_MODEL)), act(col(P_GM // D_MODEL + 1)),
                  pl.BlockSpec((1, M_V_W), lambda bi, i: (0, 0)),
                  wspec, wspec],
        out_specs=act(row),
        out_shape=jax.ShapeDtypeStruct((b, t, D_MODEL), BF16),
        scratch_shapes=[pltpu.VMEM((tm, M_V_W), BF16)],
        compiler_params=_cparams(("parallel", "parallel")),
        name="merge",
    )(hf, hb, proj, proj, oa, proj, proj, proj, mh_norm_g.reshape(1, M_V_W), w_proj_m, w_proj_a)


def _out_kernel(y_ref, x_ref, gate_ref, w_ref, o_ref):
    o_ref[...] = x_ref[...] + gate_ref[...] * jnp.dot(y_ref[...], w_ref[...], preferred_element_type=F32)


def _out(y, x, gate, w_out):
    b, t, _ = x.shape
    tm = min(512, t)
    row = lambda bi, i: (bi, i, 0)
    return pl.pallas_call(
        _out_kernel,
        grid=(b, t // tm),
        in_specs=[pl.BlockSpec((None, tm, D_MODEL), row),
                  pl.BlockSpec((None, tm, D_MODEL), row),
                  pl.BlockSpec((None, 1, D_MODEL), lambda bi, i: (bi, 0, 0)),
                  pl.BlockSpec((D_MODEL, D_MODEL), lambda bi, i: (0, 0))],
        out_specs=pl.BlockSpec((None, tm, D_MODEL), row),
        out_shape=jax.ShapeDtypeStruct((b, t, D_MODEL), F32),
        compiler_params=_cparams(("parallel", "parallel")),
        name="out",
    )(y, x, gate, w_out)


def _layout_w_in(w):
    km, vm, ckv, kr = w[:, _O_KM:_O_VM], w[:, _O_VM:_O_GT], w[:, _O_CKV:_O_KR], w[:, _O_KR:_O_QM]
    main = jnp.concatenate([vm, w[:, _O_OM:_O_ZM], w[:, _O_ZM:_O_QA], w[:, _O_ZA:_O_GM], w[:, _O_GM:_O_END],
                            w[:, _O_QA:_O_ZA], km, w[:, _O_QM:_O_OM], ckv], axis=1).astype(BF16)
    ctx = jnp.concatenate([vm, km, ckv], axis=1).astype(BF16)
    small = jnp.concatenate([w[:, _O_GT:_O_CKV], jnp.zeros((D_MODEL, S_KR - M_GATE_W), w.dtype), kr, kr],
                            axis=1).astype(BF16)
    return main, ctx, small


def _rope_tables(seq):
    pos = np.arange(seq)
    lane = np.arange(LANES) % A_ROPE
    axis = lane // (2 * ROPE_FREQS)
    half = (lane % (2 * ROPE_FREQS)) // ROPE_FREQS
    freqs = ROPE_THETA ** (-np.arange(ROPE_FREQS, dtype=np.float64) / ROPE_FREQS)
    coord = np.where((axis == 0)[None, :], (pos // GRID_W)[:, None], (pos % GRID_W)[:, None]).astype(np.float64)
    ang = coord * freqs[lane % ROPE_FREQS][None, :]
    sign = np.where(half == 0, -1.0, 1.0)[None, :]
    return jnp.asarray(np.cos(ang), F32), jnp.asarray(np.sin(ang) * sign, F32)


def _layer(x, c, ctx, c_ctx, ada_w, ada_b, norm_g, w_in, conv_w, conv_b, gate_b, mh_norm_g, q_norm_g, k_norm_g,
           kv_norm_g, w_uk, w_uv, w_proj_m, w_proj_a, w_out):
    b, t, _ = x.shape
    tc = ctx.shape[1]
    assert tc == MLSTM_CHUNK and t % MLSTM_CHUNK == 0 and t % GRID_W == 0

    c8 = jnp.zeros((8, D_MODEL), F32).at[:b].set(c).at[b].set(c_ctx)
    mod = _adaln(c8, ada_w, ada_b)
    shift, scale, gate = mod[:, :D_MODEL], mod[:, D_MODEL:2 * D_MODEL], mod[:, 2 * D_MODEL:]
    per_b = lambda a: a[:b].reshape(b, 1, D_MODEL)
    per_c = lambda a: jnp.broadcast_to(a[b].reshape(1, 1, D_MODEL), (b, 1, D_MODEL))

    w_main, w_ctx, w_small = _layout_w_in(w_in)
    proj, small = _proj(x, per_b(scale), per_b(shift), norm_g, w_main, w_small, P_COLS, 2560)
    proj_c, small_c = _proj(ctx, per_c(scale), per_c(shift), norm_g, w_ctx, w_small, C_COLS, C_COLS // 2)

    cw_q, cw_k = conv_w[:, :M_QK_W], conv_w[:, M_QK_W:]
    cb_q, cb_k = conv_b[:M_QK_W], conv_b[M_QK_W:]
    q_m = _conv(proj, P_QM, cw_q, cb_q, M_DQK ** -0.5, False)
    kt_m = _conv(proj, P_KM, cw_k, cb_k, 1.0, True)
    kt_c = _conv(proj_c, C_KM, cw_k, cb_k, 1.0, True)
    gate_bias = jnp.zeros((1, LANES), F32).at[0, :M_GATE_W].set(gate_b)
    hf, hb = _mlstm(q_m, kt_m, proj, small, kt_c, proj_c, small_c, gate_bias)

    cos_t, sin_t = _rope_tables(t)
    k_a, v_a, q_a = _mla_prep(proj, small, proj_c, small_c, w_uk, w_uv, kv_norm_g, k_norm_g, q_norm_g, cos_t, sin_t)
    o_a = _attention(q_a, k_a, v_a, tc)

    y = _merge(hf, hb, proj, o_a, mh_norm_g, w_proj_m.astype(BF16), w_proj_a.astype(BF16))
    return _out(y, x, per_b(gate), w_out.astype(BF16))


def kernel(x, c, ctx, c_ctx, ada_w, ada_b, norm_g, w_in, conv_w, conv_b, gate_b, mh_norm_g, q_norm_g, k_norm_g,
           kv_norm_g, w_uk, w_uv, w_proj_m, w_proj_a, w_out):
    assert ada_w.shape[0] == 1, "single-layer block"
    return _layer(x, c, ctx, c_ctx, ada_w[0], ada_b[0], norm_g[0], w_in[0], conv_w[0], conv_b[0], gate_b[0],
                  mh_norm_g[0], q_norm_g[0], k_norm_g[0], kv_norm_g[0], w_uk[0], w_uv[0], w_proj_m[0],
                  w_proj_a[0], w_out[0])
```

```python
import functools

import numpy as np
import jax
import jax.numpy as jnp
from jax import lax
from jax.experimental import pallas as pl
from jax.experimental.pallas import tpu as pltpu

F32 = jnp.float32
BF16 = jnp.bfloat16

D_MODEL = 2048
GRID_W = 64
EPS = 1e-6

M_HEADS = 8
M_DQK = 128
M_DV = 256
M_CONV = 5
A_HEADS = 16
A_NOPE = 128
A_ROPE = 64
A_QK = A_NOPE + A_ROPE
A_DV = 128
A_VX = 256
A_QKX = 256
KV_RANK = 512
ROPE_FREQS = A_ROPE // 4
ROPE_THETA = 10000.0

M_QK_W = M_HEADS * M_DQK
M_V_W = M_HEADS * M_DV
M_GATE_W = 4 * M_HEADS
A_Q_W = A_HEADS * A_QK
A_V_W = A_HEADS * A_DV

_O_KM = 0
_O_VM = _O_KM + M_QK_W
_O_GT = _O_VM + M_V_W
_O_CKV = _O_GT + M_GATE_W
_O_KR = _O_CKV + KV_RANK
_O_QM = _O_KR + A_ROPE
_O_OM = _O_QM + M_QK_W
_O_ZM = _O_OM + M_V_W
_O_QA = _O_ZM + M_V_W
_O_ZA = _O_QA + A_Q_W
_O_GM = _O_ZA + A_V_W
_O_END = _O_GM + 2 * D_MODEL

P_VM = 0
P_OM = P_VM + M_V_W
P_ZM = P_OM + M_V_W
P_ZA = P_ZM + M_V_W
P_GM = P_ZA + A_V_W
P_QA = P_GM + 2 * D_MODEL
P_KM = P_QA + A_Q_W
P_QM = P_KM + M_QK_W
P_CKV = P_QM + M_QK_W
P_COLS = P_CKV + KV_RANK
C_VM = 0
C_KM = C_VM + M_V_W
C_CKV = C_KM + M_QK_W
C_COLS = C_CKV + KV_RANK
S_GT = 0
S_KR = 128
S_COLS = 256

LANES = 128
MLSTM_CHUNK = 256
MLSTM_CHUNKS_PER_STEP = 2
NEG_BIG = -1e30
LOG2_E = 1.4426950408889634
VMEM_LIMIT = 60 * 1024 * 1024


def _cparams(sem):
    return pltpu.CompilerParams(dimension_semantics=sem, vmem_limit_bytes=VMEM_LIMIT)


def _sigmoid(a):
    return 0.5 * jnp.tanh(0.5 * a) + 0.5


def _silu(a):
    half = 0.5 * a
    return half * jnp.tanh(half) + half


def _adaln_kernel(c_ref, w_ref, b_ref, o_ref):
    s = _silu(c_ref[...])
    o_ref[...] = jnp.dot(s.astype(BF16), w_ref[...].astype(BF16), preferred_element_type=F32) + b_ref[...]


def _adaln(c8, ada_w, ada_b):
    n = ada_w.shape[1]
    tn = 1024
    return pl.pallas_call(
        _adaln_kernel,
        grid=(n // tn,),
        in_specs=[pl.BlockSpec((8, D_MODEL), lambda j: (0, 0)),
                  pl.BlockSpec((D_MODEL, tn), lambda j: (0, j)),
                  pl.BlockSpec((1, tn), lambda j: (0, j))],
        out_specs=pl.BlockSpec((8, tn), lambda j: (0, j)),
        out_shape=jax.ShapeDtypeStruct((8, n), F32),
        compiler_params=_cparams(("parallel",)),
        name="adaln",
    )(c8, ada_w, ada_b.reshape(1, n))


def _proj_kernel(x_ref, sc_ref, sh_ref, g_ref, w_ref, ws_ref, o_ref, os_ref, h_ref):
    @pl.when(pl.program_id(2) == 0)
    def _():
        x = x_ref[...]
        ms = jnp.mean(x * x, axis=-1, keepdims=True)
        y = x * lax.rsqrt(ms + EPS) * g_ref[...]
        h = (y * (1.0 + sc_ref[...]) + sh_ref[...]).astype(BF16)
        h_ref[...] = h
        os_ref[...] = jnp.dot(h, ws_ref[...], preferred_element_type=F32)

    o_ref[...] = jnp.dot(h_ref[...], w_ref[...], preferred_element_type=F32).astype(o_ref.dtype)


def _proj(x, scale, shift, norm_g, w_main, w_small, n_cols, tn, w_blocks=None):
    b, t, _ = x.shape
    tm = min(1024, t)
    if w_blocks is None:
        w_block = lambda j: j
    else:
        assert len(w_blocks) * tn == n_cols
        w_block = lambda j: functools.reduce(lambda acc, kv: jnp.where(j == kv[0], kv[1], acc), enumerate(w_blocks), 0)
    return pl.pallas_call(
        _proj_kernel,
        grid=(b, t // tm, n_cols // tn),
        in_specs=[pl.BlockSpec((None, tm, D_MODEL), lambda bi, i, j: (bi, i, 0)),
                  pl.BlockSpec((None, 1, D_MODEL), lambda bi, i, j: (bi, 0, 0)),
                  pl.BlockSpec((None, 1, D_MODEL), lambda bi, i, j: (bi, 0, 0)),
                  pl.BlockSpec((1, D_MODEL), lambda bi, i, j: (0, 0)),
                  pl.BlockSpec((D_MODEL, tn), lambda bi, i, j: (0, w_block(j))),
                  pl.BlockSpec((D_MODEL, S_COLS), lambda bi, i, j: (0, 0))],
        out_specs=[pl.BlockSpec((None, tm, tn), lambda bi, i, j: (bi, i, j)),
                   pl.BlockSpec((None, tm, S_COLS), lambda bi, i, j: (bi, i, 0))],
        out_shape=[jax.ShapeDtypeStruct((b, t, n_cols), BF16),
                   jax.ShapeDtypeStruct((b, t, S_COLS), F32)],
        scratch_shapes=[pltpu.VMEM((tm, D_MODEL), BF16)],
        compiler_params=_cparams(("parallel", "parallel", "arbitrary")),
        name="proj",
    )(x, scale, shift, norm_g.reshape(1, D_MODEL), w_main, w_small)


CONV_HALO = 16


def _conv_kernel(prev_ref, cur_ref, next_ref, w_ref, b_ref, o_ref, buf_ref, *, tt, nt, out_scale, transpose):
    i = pl.program_id(1)
    buf_ref[CONV_HALO:CONV_HALO + tt, :] = cur_ref[...].astype(F32)
    buf_ref[0:CONV_HALO, :] = prev_ref[...].astype(F32) * (i > 0).astype(F32)
    buf_ref[CONV_HALO + tt:2 * CONV_HALO + tt, :] = next_ref[...].astype(F32) * (i < nt - 1).astype(F32)
    acc = jnp.broadcast_to(b_ref[...], (tt, cur_ref.shape[-1]))
    for k in range(M_CONV):
        lo = CONV_HALO - M_CONV // 2 + k
        acc = acc + w_ref[k:k + 1, :] * buf_ref[lo:lo + tt, :]
    y = _silu(acc)
    if out_scale != 1.0:
        y = y * out_scale
    o_ref[...] = (y.T if transpose else y).astype(o_ref.dtype)


def _conv(src, col_off, w, bias, out_scale, transpose):
    b, t, _ = src.shape
    cw = 512
    tt = min(512, t)
    nt = t // tt
    cb = col_off // cw
    hb = tt // CONV_HALO
    nhalo = t // CONV_HALO
    if transpose:
        out_shape = jax.ShapeDtypeStruct((b, M_QK_W, t), BF16)
        out_specs = pl.BlockSpec((None, cw, tt), lambda bi, i, c: (bi, c, i))
    else:
        out_shape = jax.ShapeDtypeStruct((b, t, M_QK_W), BF16)
        out_specs = pl.BlockSpec((None, tt, cw), lambda bi, i, c: (bi, i, c))
    return pl.pallas_call(
        functools.partial(_conv_kernel, tt=tt, nt=nt, out_scale=out_scale, transpose=transpose),
        grid=(b, nt, M_QK_W // cw),
        in_specs=[pl.BlockSpec((None, CONV_HALO, cw), lambda bi, i, c: (bi, jnp.maximum(i * hb - 1, 0), cb + c)),
                  pl.BlockSpec((None, tt, cw), lambda bi, i, c: (bi, i, cb + c)),
                  pl.BlockSpec((None, CONV_HALO, cw), lambda bi, i, c: (bi, jnp.minimum((i + 1) * hb, nhalo - 1), cb + c)),
                  pl.BlockSpec((M_CONV, cw), lambda bi, i, c: (0, c)),
                  pl.BlockSpec((1, cw), lambda bi, i, c: (0, c))],
        out_specs=out_specs,
        out_shape=out_shape,
        scratch_shapes=[pltpu.VMEM((tt + 2 * CONV_HALO, cw), F32)],
        compiler_params=_cparams(("parallel", "parallel", "parallel")),
        name="conv_t" if transpose else "conv",
    )(src, src, src, w, bias.reshape(1, M_QK_W))


def _mlstm_unit(q, kt, v, g_col, g_row, ig_row, b_tot, mask, ones, c_ref, n_ref, m_ref, idx):
    m_prev = m_ref[idx][0:1, 0:1]
    c_prev = c_ref[idx]
    n_prev = n_ref[idx]
    a_row = ig_row - g_row
    h_out = None
    if q is not None:
        am = jnp.where(mask, a_row, NEG_BIG)
        c_col = jnp.maximum(m_prev, jnp.max(am, axis=-1, keepdims=True))
        c_b = jnp.broadcast_to(c_col, mask.shape)
        s = (jnp.dot(q, kt, preferred_element_type=F32) * jnp.exp2(am - c_b)).astype(BF16)
        w_state = jnp.exp2(m_prev - c_b)
        num = (jnp.dot(s, v, preferred_element_type=F32)
               + w_state * jnp.dot(q, c_prev.astype(BF16), preferred_element_type=F32))
        den = (jnp.dot(s, ones, preferred_element_type=F32)
               + w_state[:, :LANES] * jnp.dot(q, n_prev.astype(BF16), preferred_element_type=F32))
        den = jnp.maximum(jnp.abs(den), jnp.exp2(-(g_col + c_col)))
        h_out = num / jnp.concatenate([den] * (num.shape[-1] // LANES), axis=-1)
    w_row = b_tot + a_row
    m_new = jnp.maximum(b_tot + m_prev, jnp.max(w_row, axis=-1, keepdims=True))
    decay = jnp.exp2(b_tot + m_prev - m_new)
    kw = kt * jnp.exp2(w_row - m_new).astype(BF16)
    c_ref[idx] = decay * c_prev + jnp.dot(kw, v, preferred_element_type=F32)
    n_ref[idx] = decay * n_prev + jnp.dot(kw, ones, preferred_element_type=F32)
    m_ref[idx] = jnp.broadcast_to(m_new, m_ref.shape[1:])
    return h_out


def _mlstm_gates(gt_ref, gb_ref, tri, rs):
    a = gt_ref[rs, :] + gb_ref[...]
    lane = lax.broadcasted_iota(jnp.int32, a.shape, 1)
    is_forget = ((lane // M_HEADS) % 2) == 1
    act = jnp.where(is_forget, jax.nn.log_sigmoid(a), a) * LOG2_E
    cum = jnp.dot(tri, act, preferred_element_type=F32, precision=lax.Precision.HIGHEST)
    return cum, act.T, cum.T


def _mlstm_kernel(qf_ref, ktf_ref, vf_ref, gf_ref,
                  qb_ref, ktb_ref, vb_ref, gb_ref,
                  ktc_ref, vc_ref, gc_ref, gbias_ref,
                  hf_ref, hb_ref, c_ref, n_ref, m_ref):
    i = pl.program_id(1)
    L = MLSTM_CHUNK
    rows = lax.broadcasted_iota(jnp.int32, (L, L), 0)
    cols = lax.broadcasted_iota(jnp.int32, (L, L), 1)
    lower = cols <= rows
    upper = cols >= rows
    ones = jnp.ones((L, LANES), BF16)

    def run(q_ref, kt_ref, v_ref, g_ref, h_ref, direction, sub):
        causal = direction == 0
        mask = lower if causal else upper
        rs = slice(sub * L, (sub + 1) * L)
        cum, act_t, cum_t = _mlstm_gates(g_ref, gbias_ref, mask.astype(F32), rs)
        for hd in range(M_HEADS):
            ci = 2 * M_HEADS * direction + hd
            cf = ci + M_HEADS
            g_col = cum[:, cf:cf + 1]
            b_tot = g_col[L - 1:L, :] if causal else g_col[0:1, :]
            ks = slice(hd * M_DQK, (hd + 1) * M_DQK)
            vs = slice(hd * M_DV, (hd + 1) * M_DV)
            q = None if q_ref is None else q_ref[rs, ks]
            h = _mlstm_unit(q, kt_ref[ks, rs], v_ref[rs, vs], g_col, cum_t[cf:cf + 1, :], act_t[ci:ci + 1, :],
                            b_tot, mask, ones, c_ref, n_ref, m_ref, direction * M_HEADS + hd)
            if h is not None:
                h_ref[rs, vs] = h.astype(h_ref.dtype)

    @pl.when(i == 0)
    def _():
        c_ref[...] = jnp.zeros_like(c_ref)
        n_ref[...] = jnp.zeros_like(n_ref)
        m_ref[...] = jnp.zeros_like(m_ref)
        run(None, ktc_ref, vc_ref, gc_ref, None, 0, 0)
        run(None, ktc_ref, vc_ref, gc_ref, None, 1, 0)

    @pl.when(i > 0)
    def _():
        for sub in range(MLSTM_CHUNKS_PER_STEP):
            run(qf_ref, ktf_ref, vf_ref, gf_ref, hf_ref, 0, sub)
            run(qb_ref, ktb_ref, vb_ref, gb_ref, hb_ref, 1, MLSTM_CHUNKS_PER_STEP - 1 - sub)


def _mlstm(q, kt, proj, small, ktc, proj_c, small_c, gate_bias):
    b, t, _ = q.shape
    L = MLSTM_CHUNK
    rows = MLSTM_CHUNKS_PER_STEP * L
    assert t % rows == 0
    nb = t // rows
    fwd = lambda bi, i: (bi, jnp.maximum(i - 1, 0), 0)
    bwd = lambda bi, i: (bi, jnp.minimum(nb - i, nb - 1), 0)
    fwd_t = lambda bi, i: (bi, 0, jnp.maximum(i - 1, 0))
    bwd_t = lambda bi, i: (bi, 0, jnp.minimum(nb - i, nb - 1))
    ctx = lambda bi, i: (bi, 0, 0)

    def specs(rm, tm_):
        return [pl.BlockSpec((None, rows, M_QK_W), rm), pl.BlockSpec((None, M_QK_W, rows), tm_),
                pl.BlockSpec((None, rows, M_V_W), rm), pl.BlockSpec((None, rows, LANES), rm)]

    in_specs = (specs(fwd, fwd_t) + specs(bwd, bwd_t)
                + [pl.BlockSpec((None, M_QK_W, L), ctx), pl.BlockSpec((None, L, M_V_W), ctx),
                   pl.BlockSpec((None, L, LANES), ctx), pl.BlockSpec((1, LANES), lambda bi, i: (0, 0))])
    return pl.pallas_call(
        _mlstm_kernel,
        grid=(b, nb + 1),
        in_specs=in_specs,
        out_specs=[pl.BlockSpec((None, rows, M_V_W), fwd), pl.BlockSpec((None, rows, M_V_W), bwd)],
        out_shape=[jax.ShapeDtypeStruct((b, t, M_V_W), BF16)] * 2,
        scratch_shapes=[pltpu.VMEM((2 * M_HEADS, M_DQK, M_DV), F32),
                        pltpu.VMEM((2 * M_HEADS, M_DQK, LANES), F32),
                        pltpu.VMEM((2 * M_HEADS, 8, LANES), F32)],
        compiler_params=_cparams(("parallel", "arbitrary")),
        name="mlstm",
    )(q, kt, proj, small, q, kt, proj, small, ktc, proj_c, small_c, gate_bias)


def _rope_partner_index():
    lane = np.arange(LANES)
    return np.where(lane % (2 * ROPE_FREQS) < ROPE_FREQS, lane + ROPE_FREQS, lane - ROPE_FREQS)


def _mla_prep_kernel(ckv_x_ref, sm_x_ref, ckv_c_ref, sm_c_ref, qa_ref,
                     wuk_ref, wuv_ref, kvg_ref, kgn_ref, kgr_ref, qgw_ref, qgs_ref, cos_ref, sin_ref,
                     pair_ref, sel_ref, half_ref, swap_ref,
                     k_ref, v_ref, q_ref):
    i = pl.program_id(1)
    rows = ckv_x_ref.shape[0]
    lane_id = lax.broadcasted_iota(jnp.int32, (rows, LANES), 1)
    low_half = lane_id < A_ROPE
    low_half_of_pair = (lane_id % (2 * ROPE_FREQS)) < ROPE_FREQS
    eps_dim = A_QK * EPS

    def keys_values(ckv_ref, sm_ref, rotate):
        ckv = ckv_ref[...].astype(F32)
        cn = (ckv * lax.rsqrt(jnp.mean(ckv * ckv, axis=-1, keepdims=True) + EPS) * kvg_ref[...]).astype(BF16)
        v_ref[...] = jnp.dot(cn, wuv_ref[...], preferred_element_type=F32).astype(v_ref.dtype)
        kn = jnp.dot(cn, wuk_ref[...], preferred_element_type=F32)
        sq = (kn * kn).astype(BF16)
        pair = pair_ref[...]
        ss = jnp.concatenate([jnp.dot(sq[:, g * 2 * A_NOPE:(g + 1) * 2 * A_NOPE], pair, preferred_element_type=F32)
                              for g in range(A_HEADS // 2)], axis=-1)
        kr2 = sm_ref[:, S_KR:S_KR + LANES]
        kr_ss = jnp.dot((kr2 * kr2).astype(BF16), half_ref[...], preferred_element_type=F32)
        r = lax.rsqrt(ss + jnp.concatenate([kr_ss + eps_dim] * (A_HEADS // 2), axis=-1))
        kns = kn * r * kgn_ref[...]
        krg = kr2 * kgr_ref[...]
        if rotate:
            partner = jnp.where(low_half_of_pair, pltpu.roll(krg, LANES - ROPE_FREQS, 1), pltpu.roll(krg, ROPE_FREQS, 1))
            krg = krg * cos_ref[...] + partner * sin_ref[...]
        for hd in range(A_HEADS):
            hs = slice(hd * A_NOPE, (hd + 1) * A_NOPE)
            kn_h = kns[:, hs]
            kr_h = krg * r[:, hs]
            if hd % 2 == 0:
                k_ref[hd, :, :LANES] = kn_h.astype(k_ref.dtype)
                k_ref[hd, :, LANES:] = jnp.where(low_half, kr_h, 0.0).astype(k_ref.dtype)
            else:
                k_ref[hd, :, :LANES] = jnp.where(low_half, 0.0, kn_h).astype(k_ref.dtype)
                k_ref[hd, :, LANES:] = jnp.where(low_half, kn_h, kr_h).astype(k_ref.dtype)

    @pl.when(i == 0)
    def _():
        keys_values(ckv_c_ref, sm_c_ref, False)

    @pl.when(i > 0)
    def _():
        keys_values(ckv_x_ref, sm_x_ref, True)
        cos_e, sin_e = cos_ref[...], sin_ref[...]
        cos_o, sin_o = jnp.where(low_half, 1.0, cos_e), jnp.where(low_half, 0.0, sin_e)
        swap = swap_ref[...]
        for hd in range(A_HEADS):
            odd = hd % 2
            lo = (hd // 2) * 3 * LANES + odd * LANES
            win = qa_ref[:, lo:lo + A_QKX]
            wf = win.astype(F32)
            ss = jnp.dot((wf * wf).astype(BF16), sel_ref[odd], preferred_element_type=F32)
            r = lax.rsqrt(ss + eps_dim)
            x1 = wf[:, :LANES] * qgw_ref[odd:odd + 1, :LANES]
            x2 = wf[:, LANES:] * qgw_ref[odd:odd + 1, LANES:]
            p2 = jnp.dot(win[:, LANES:], swap, preferred_element_type=F32) * qgs_ref[odd:odd + 1, :]
            rot2 = x2 * (cos_o if odd else cos_e) + p2 * (sin_o if odd else sin_e)
            q_ref[hd, :, :LANES] = (x1 * r).astype(q_ref.dtype)
            q_ref[hd, :, LANES:] = (rot2 * r).astype(q_ref.dtype)


def _mla_prep(proj, small, proj_c, small_c, w_uk, w_uv, kv_norm_g, k_norm_g, q_norm_g, cos_t, sin_t):
    b, t, _ = proj.shape
    tc = proj_c.shape[1]
    tr = tc
    nx = t // tr
    xrow = lambda bi, i: (bi, jnp.maximum(i - 1, 0), 0)
    const = lambda *shape: pl.BlockSpec(shape, lambda bi, i: (0,) * len(shape))

    halves = lambda a: a.reshape(a.shape[:-1] + (A_HEADS // 2, 2, 2, A_NOPE // 2))
    swap_odd = lambda a: jnp.concatenate([halves(a)[..., 0:1, :, :], halves(a)[..., 1:2, ::-1, :]],
                                         axis=-3).reshape(a.shape)
    w_uk_l = swap_odd(w_uk)
    gk_n, gk_r = k_norm_g[:A_NOPE] * A_QK ** 0.5, k_norm_g[A_NOPE:] * A_QK ** 0.5
    gq_n, gq_r = q_norm_g[:A_NOPE] * LOG2_E, q_norm_g[A_NOPE:] * LOG2_E
    kgn = swap_odd(jnp.tile(gk_n, A_HEADS)).reshape(1, A_HEADS * A_NOPE)
    kgr = jnp.tile(gk_r, 2).reshape(1, LANES)
    zeros = jnp.zeros((A_ROPE,), F32)
    qgw = jnp.stack([jnp.concatenate([gq_n, gq_r, zeros]), jnp.concatenate([zeros, gq_n, gq_r])])
    partner = _rope_partner_index()
    qgs = qgw[:, LANES:][:, partner]

    lane = np.arange(A_QKX)
    pair = (lane[:, None] // A_NOPE == lane[None, :] // A_NOPE).astype(np.float32)
    sel = np.stack([np.broadcast_to((lane < A_QK)[:, None], (A_QKX, LANES)),
                    np.broadcast_to((lane >= A_QKX - A_QK)[:, None], (A_QKX, LANES))]).astype(np.float32)
    half = np.broadcast_to((np.arange(LANES) < A_ROPE)[:, None], (LANES, A_QKX)).astype(np.float32)
    swap = np.zeros((LANES, LANES), np.float32)
    swap[partner, np.arange(LANES)] = 1.0
    as_bf16 = lambda a: jnp.asarray(a, BF16)

    return pl.pallas_call(
        _mla_prep_kernel,
        grid=(b, nx + 1),
        in_specs=[pl.BlockSpec((None, tr, KV_RANK), lambda bi, i: (bi, jnp.maximum(i - 1, 0), P_CKV // KV_RANK)),
                  pl.BlockSpec((None, tr, S_COLS), xrow),
                  pl.BlockSpec((None, tr, KV_RANK), lambda bi, i: (bi, 0, C_CKV // KV_RANK)),
                  pl.BlockSpec((None, tr, S_COLS), lambda bi, i: (bi, 0, 0)),
                  pl.BlockSpec((None, tr, A_Q_W), lambda bi, i: (bi, jnp.maximum(i - 1, 0), P_QA // A_Q_W)),
                  const(KV_RANK, A_HEADS * A_NOPE), const(KV_RANK, A_V_W), const(1, KV_RANK),
                  const(1, A_HEADS * A_NOPE), const(1, LANES), const(2, A_QKX), const(2, LANES),
                  pl.BlockSpec((tr, LANES), lambda bi, i: (jnp.maximum(i - 1, 0), 0)),
                  pl.BlockSpec((tr, LANES), lambda bi, i: (jnp.maximum(i - 1, 0), 0)),
                  const(A_QKX, A_QKX), const(2, A_QKX, LANES), const(LANES, A_QKX), const(LANES, LANES)],
        out_specs=[pl.BlockSpec((None, A_HEADS, tr, A_QKX), lambda bi, i: (bi, 0, i, 0)),
                   pl.BlockSpec((None, tr, A_V_W), lambda bi, i: (bi, i, 0)),
                   pl.BlockSpec((None, A_HEADS, tr, A_QKX), lambda bi, i: (bi, 0, jnp.maximum(i - 1, 0), 0))],
        out_shape=[jax.ShapeDtypeStruct((b, A_HEADS, tc + t, A_QKX), BF16),
                   jax.ShapeDtypeStruct((b, tc + t, A_V_W), BF16),
                   jax.ShapeDtypeStruct((b, A_HEADS, t, A_QKX), BF16)],
        compiler_params=_cparams(("parallel", "arbitrary")),
        name="mla_prep",
    )(proj, small, proj_c, small_c, proj, w_uk_l.astype(BF16), w_uv.astype(BF16), kv_norm_g.reshape(1, KV_RANK),
      kgn, kgr, qgw, qgs, cos_t, sin_t, as_bf16(pair), as_bf16(sel), as_bf16(half), as_bf16(swap))


def _attn_kernel(q_ref, k_ref, v_ref, onecol_ref, o_ref, *, key_blocks):
    q = q_ref[...]
    m = acc = None
    for lo, hi in key_blocks:
        s = lax.dot_general(q, k_ref[lo:hi, :], (((1,), (1,)), ((), ())), preferred_element_type=F32)
        m_blk = jnp.max(s, axis=-1, keepdims=True)
        m_new = m_blk if m is None else jnp.maximum(m, m_blk)
        p = jnp.exp2(s - m_new).astype(BF16)
        v_ext = jnp.concatenate([v_ref[lo:hi, :], onecol_ref[:hi - lo, :]], axis=1)
        pv = jnp.dot(p, v_ext, preferred_element_type=F32)
        acc = pv if acc is None else acc * jnp.exp2(m - m_new) + pv
        m = m_new
    o_ref[...] = (acc[:, :A_DV] / acc[:, A_DV:A_DV + 1]).astype(o_ref.dtype)


ATTN_KEY_BLOCK = 256


def _attention(q, k, v, n_ctx):
    b, h, s, _ = q.shape
    tk = k.shape[2]
    tq = min(1024, s)
    key_blocks = [(lo, lo + ATTN_KEY_BLOCK) for lo in range(n_ctx, tk, ATTN_KEY_BLOCK)] + [(0, n_ctx)]
    rows = max(hi - lo for lo, hi in key_blocks)
    onecol = np.zeros((rows, A_VX - A_DV), np.float32)
    onecol[:, 0] = 1.0
    return pl.pallas_call(
        functools.partial(_attn_kernel, key_blocks=tuple(key_blocks)),
        grid=(b, h, s // tq),
        in_specs=[pl.BlockSpec((None, None, tq, A_QKX), lambda bi, hi, i: (bi, hi, i, 0)),
                  pl.BlockSpec((None, None, tk, A_QKX), lambda bi, hi, i: (bi, hi, 0, 0)),
                  pl.BlockSpec((None, tk, A_DV), lambda bi, hi, i: (bi, 0, hi)),
                  pl.BlockSpec((rows, A_VX - A_DV), lambda bi, hi, i: (0, 0))],
        out_specs=pl.BlockSpec((None, tq, A_DV), lambda bi, hi, i: (bi, i, hi)),
        out_shape=jax.ShapeDtypeStruct((b, s, A_V_W), BF16),
        compiler_params=_cparams(("parallel", "parallel", "parallel")),
        name="attention",
    )(q, k, v, jnp.asarray(onecol, BF16))


def _merge_kernel(hf_ref, hb_ref, om_ref, zm_ref, oa_ref, za_ref, gm_ref, ga_ref, mhg_ref,
                  wm_ref, wa_ref, o_ref, hm_ref):
    for hd in range(M_HEADS):
        vs = slice(hd * M_DV, (hd + 1) * M_DV)
        h = hf_ref[:, vs].astype(F32) + hb_ref[:, vs].astype(F32)
        hn = h * lax.rsqrt(jnp.mean(h * h, axis=-1, keepdims=True) + EPS) * mhg_ref[:, vs]
        gated = hn * _sigmoid(om_ref[:, vs].astype(F32)) * _silu(zm_ref[:, vs].astype(F32))
        hm_ref[:, vs] = gated.astype(BF16)
    p_m = jnp.dot(hm_ref[...], wm_ref[...], preferred_element_type=F32)
    oa = (oa_ref[...].astype(F32) * _silu(za_ref[...].astype(F32))).astype(BF16)
    p_a = jnp.dot(oa, wa_ref[...], preferred_element_type=F32)
    y = _sigmoid(gm_ref[...].astype(F32)) * p_m + _sigmoid(ga_ref[...].astype(F32)) * p_a
    o_ref[...] = y.astype(o_ref.dtype)


def _merge(hf, hb, proj, oa, mh_norm_g, w_proj_m, w_proj_a):
    b, t, _ = hf.shape
    tm = 256
    row = lambda bi, i: (bi, i, 0)
    col = lambda c: (lambda bi, i: (bi, i, c))
    wspec = pl.BlockSpec((D_MODEL, D_MODEL), lambda bi, i: (0, 0))
    act = lambda im: pl.BlockSpec((None, tm, D_MODEL), im)
    return pl.pallas_call(
        _merge_kernel,
        grid=(b, t // tm),
        in_specs=[act(row), act(row),
                  act(col(P_OM // D_MODEL)), act(col(P_ZM // D_MODEL)),
                  act(row), act(col(P_ZA // D_MODEL)),
                  act(col(P_GM // D_MODEL)), act(col(P_GM // D_MODEL + 1)),
                  pl.BlockSpec((1, M_V_W), lambda bi, i: (0, 0)),
                  wspec, wspec],
        out_specs=act(row),
        out_shape=jax.ShapeDtypeStruct((b, t, D_MODEL), BF16),
        scratch_shapes=[pltpu.VMEM((tm, M_V_W), BF16)],
        compiler_params=_cparams(("parallel", "parallel")),
        name="merge",
    )(hf, hb, proj, proj, oa, proj, proj, proj, mh_norm_g.reshape(1, M_V_W), w_proj_m, w_proj_a)


def _out_kernel(y_ref, x_ref, gate_ref, w_ref, o_ref):
    o_ref[...] = x_ref[...] + gate_ref[...] * jnp.dot(y_ref[...], w_ref[...], preferred_element_type=F32)


def _out(y, x, gate, w_out):
    b, t, _ = x.shape
    tm = min(512, t)
    row = lambda bi, i: (bi, i, 0)
    return pl.pallas_call(
        _out_kernel,
        grid=(b, t // tm),
        in_specs=[pl.BlockSpec((None, tm, D_MODEL), row),
                  pl.BlockSpec((None, tm, D_MODEL), row),
                  pl.BlockSpec((None, 1, D_MODEL), lambda bi, i: (bi, 0, 0)),
                  pl.BlockSpec((D_MODEL, D_MODEL), lambda bi, i: (0, 0))],
        out_specs=pl.BlockSpec((None, tm, D_MODEL), row),
        out_shape=jax.ShapeDtypeStruct((b, t, D_MODEL), F32),
        compiler_params=_cparams(("parallel", "parallel")),
        name="out",
    )(y, x, gate, w_out)


def _layout_w_in(w):
    km, vm, ckv, kr = w[:, _O_KM:_O_VM], w[:, _O_VM:_O_GT], w[:, _O_CKV:_O_KR], w[:, _O_KR:_O_QM]
    main = jnp.concatenate([vm, w[:, _O_OM:_O_ZM], w[:, _O_ZM:_O_QA], w[:, _O_ZA:_O_GM], w[:, _O_GM:_O_END],
                            w[:, _O_QA:_O_ZA], km, w[:, _O_QM:_O_OM], ckv], axis=1).astype(BF16)
    small = jnp.concatenate([w[:, _O_GT:_O_CKV], jnp.zeros((D_MODEL, S_KR - M_GATE_W), w.dtype), kr, kr],
                            axis=1).astype(BF16)
    return main, small


def _rope_tables(seq):
    pos = np.arange(seq)
    lane = np.arange(LANES) % A_ROPE
    axis = lane // (2 * ROPE_FREQS)
    half = (lane % (2 * ROPE_FREQS)) // ROPE_FREQS
    freqs = ROPE_THETA ** (-np.arange(ROPE_FREQS, dtype=np.float64) / ROPE_FREQS)
    coord = np.where((axis == 0)[None, :], (pos // GRID_W)[:, None], (pos % GRID_W)[:, None]).astype(np.float64)
    ang = coord * freqs[lane % ROPE_FREQS][None, :]
    sign = np.where(half == 0, -1.0, 1.0)[None, :]
    return jnp.asarray(np.cos(ang), F32), jnp.asarray(np.sin(ang) * sign, F32)


def _layer(x, c, ctx, c_ctx, ada_w, ada_b, norm_g, w_in, conv_w, conv_b, gate_b, mh_norm_g, q_norm_g, k_norm_g,
           kv_norm_g, w_uk, w_uv, w_proj_m, w_proj_a, w_out):
    b, t, _ = x.shape
    tc = ctx.shape[1]
    assert tc == MLSTM_CHUNK and t % MLSTM_CHUNK == 0 and t % GRID_W == 0

    c8 = jnp.zeros((8, D_MODEL), F32).at[:b].set(c).at[b].set(c_ctx)
    mod = _adaln(c8, ada_w, ada_b)
    shift, scale, gate = mod[:, :D_MODEL], mod[:, D_MODEL:2 * D_MODEL], mod[:, 2 * D_MODEL:]
    per_b = lambda a: a[:b].reshape(b, 1, D_MODEL)
    per_c = lambda a: jnp.broadcast_to(a[b].reshape(1, 1, D_MODEL), (b, 1, D_MODEL))

    w_main, w_small = _layout_w_in(w_in)
    proj, small = _proj(x, per_b(scale), per_b(shift), norm_g, w_main, w_small, P_COLS, 2560)
    ctx_rows = ctx.reshape(1, b * tc, D_MODEL)
    ctx_blocks = [(p0 + k) // KV_RANK for p0, width in ((P_VM, M_V_W), (P_KM, M_QK_W), (P_CKV, KV_RANK))
                  for k in range(0, width, KV_RANK)]
    proj_c, small_c = _proj(ctx_rows, per_c(scale)[:1], per_c(shift)[:1], norm_g, w_main, w_small, C_COLS, KV_RANK,
                            ctx_blocks)
    proj_c, small_c = proj_c.reshape(b, tc, C_COLS), small_c.reshape(b, tc, S_COLS)

    cw_q, cw_k = conv_w[:, :M_QK_W], conv_w[:, M_QK_W:]
    cb_q, cb_k = conv_b[:M_QK_W], conv_b[M_QK_W:]
    q_m = _conv(proj, P_QM, cw_q, cb_q, M_DQK ** -0.5, False)
    kt_m = _conv(proj, P_KM, cw_k, cb_k, 1.0, True)
    kt_c = _conv(proj_c, C_KM, cw_k, cb_k, 1.0, True)
    gate_bias = jnp.zeros((1, LANES), F32).at[0, :M_GATE_W].set(gate_b)
    hf, hb = _mlstm(q_m, kt_m, proj, small, kt_c, proj_c, small_c, gate_bias)

    cos_t, sin_t = _rope_tables(t)
    k_a, v_a, q_a = _mla_prep(proj, small, proj_c, small_c, w_uk, w_uv, kv_norm_g, k_norm_g, q_norm_g, cos_t, sin_t)
    o_a = _attention(q_a, k_a, v_a, tc)

    y = _merge(hf, hb, proj, o_a, mh_norm_g, w_proj_m.astype(BF16), w_proj_a.astype(BF16))
    return _out(y, x, per_b(gate), w_out.astype(BF16))


def kernel(x, c, ctx, c_ctx, ada_w, ada_b, norm_g, w_in, conv_w, conv_b, gate_b, mh_norm_g, q_norm_g, k_norm_g,
           kv_norm_g, w_uk, w_uv, w_proj_m, w_proj_a, w_out):
    assert ada_w.shape[0] == 1, "single-layer block"
    return _layer(x, c, ctx, c_ctx, ada_w[0], ada_b[0], norm_g[0], w_in[0], conv_w[0], conv_b[0], gate_b[0],
                  mh_norm_g[0], q_norm_g[0], k_norm_g[0], kv_norm_g[0], w_uk[0], w_uv[0], w_proj_m[0],
                  w_proj_a[0], w_out[0])
```
